```python
import math
import jax, jax.numpy as jnp
from jax import lax
import numpy as np

D_MODEL = 1024
BATCH = 8
SEQ = 4096
DEPTH = 2

N_HEADS = 8
N_KV_HEADS = 2
HEAD_DIM = 64
ROPE_DIM = HEAD_DIM // 4
ROPE_THETA = 500000.0
IDX_HEADS = 4
IDX_DIM = 64
TOPK_MAX = 256
Q_BLOCK = 128
SGU_CHUNK = 128
SGU_GROUPS = 4
SGU_WIDTH = 512
SGU_GROUP_DIM = SGU_WIDTH // SGU_GROUPS
D_FF = 4 * D_MODEL
ALPHA = (2 * DEPTH) ** 0.25
BETA = (8 * DEPTH) ** -0.25
LN_EPS = 1e-5

ATT_Q = N_HEADS * HEAD_DIM
ATT_KV = N_KV_HEADS * HEAD_DIM
IDX_Q = IDX_HEADS * IDX_DIM
IDX_K = IDX_DIM
IDX_W = IDX_HEADS
SPLITS = [ATT_Q, ATT_KV, ATT_KV, IDX_Q, IDX_K, IDX_W, SGU_WIDTH, SGU_WIDTH, D_MODEL, D_MODEL]
N_IN = int(sum(SPLITS))
SPLIT_POINTS = [int(s) for s in np.cumsum(SPLITS)[:-1]]
V_START = ATT_Q + ATT_KV
V_END = V_START + ATT_KV

kernel_name = "hybrid_dsa_gmlp_deepnorm"


def layer_norm(x, g, b):
    xf = x.astype(jnp.float32)
    mu = jnp.mean(xf, axis=-1, keepdims=True)
    var = jnp.mean(jnp.square(xf - mu), axis=-1, keepdims=True)
    return ((xf - mu) * lax.rsqrt(var + LN_EPS) * g.astype(jnp.float32) + b.astype(jnp.float32)).astype(x.dtype)


def rope_tables(positions):
    inv_freq = ROPE_THETA ** (-jnp.arange(0, ROPE_DIM, 2, dtype=jnp.float32) / ROPE_DIM)
    ang = positions.astype(jnp.float32)[..., None] * inv_freq
    return jnp.cos(ang)[:, :, None, :], jnp.sin(ang)[:, :, None, :]


def partial_rope(x, cos, sin):
    xr, xp = x[..., :ROPE_DIM], x[..., ROPE_DIM:]
    x1, x2 = xr[..., :ROPE_DIM // 2], xr[..., ROPE_DIM // 2:]
    c = cos.astype(x.dtype)
    s = sin.astype(x.dtype)
    rot = jnp.concatenate([x1 * c - x2 * s, x2 * c + x1 * s], axis=-1)
    return jnp.concatenate([rot, xp], axis=-1)


def dsa_attention(q, k, v, q_idx, k_idx, w_idx):
    B, S = q.shape[0], q.shape[1]
    topk = min(TOPK_MAX, S // 4)
    nb = S // Q_BLOCK
    rep = N_HEADS // N_KV_HEADS
    key_pos = jnp.arange(S)
    k_idx32 = k_idx.astype(jnp.float32)
    idx_scale = (IDX_HEADS ** -0.5) * (IDX_DIM ** -0.5)
    att_scale = HEAD_DIM ** -0.5

    def to_blocks(a):
        return jnp.moveaxis(a.reshape((B, nb, Q_BLOCK) + a.shape[2:]), 1, 0)

    def one_block(args):
        qb, qib, wb, blk = args
        q_pos = blk * Q_BLOCK + jnp.arange(Q_BLOCK)
        causal = key_pos[None, :] <= q_pos[:, None]
        logits = jnp.einsum('bqhd,bsd->bqhs', qib.astype(jnp.float32), k_idx32)
        scores = jnp.einsum('bqhs,bqh->bqs', jax.nn.relu(logits), wb.astype(jnp.float32) * idx_scale)
        scores = jnp.where(causal[None], scores, -jnp.inf)
        _, idx = lax.top_k(scores, topk)
        valid = idx <= q_pos[None, :, None]
        k_sel = jax.vmap(lambda a, i: a[i])(k, idx)
        v_sel = jax.vmap(lambda a, i: a[i])(v, idx)
        qg = qb.reshape(B, Q_BLOCK, N_KV_HEADS, rep, HEAD_DIM)
        s = jnp.einsum('bqgrd,bqkgd->bqgrk', qg.astype(jnp.float32), k_sel.astype(jnp.float32)) * att_scale
        s = jnp.where(valid[:, :, None, None, :], s, -jnp.inf)
        p = jax.nn.softmax(s, axis=-1).astype(v.dtype)
        o = jnp.einsum('bqgrk,bqkgd->bqgrd', p, v_sel)
        return o.reshape(B, Q_BLOCK, N_HEADS * HEAD_DIM)

    out = lax.map(one_block, (to_blocks(q), to_blocks(q_idx), to_blocks(w_idx), jnp.arange(nb)))
    return jnp.moveaxis(out, 0, 1).reshape(B, S, N_HEADS * HEAD_DIM)


def spatial_gating(u, v, w_s, b_s, g_v, b_v):
    B, S = v.shape[0], v.shape[1]
    v = layer_norm(v, g_v, b_v)
    nc = S // SGU_CHUNK
    vc = v.reshape(B, nc, SGU_CHUNK, SGU_GROUPS, SGU_GROUP_DIM)
    mask = jnp.tril(jnp.ones((SGU_CHUNK, SGU_CHUNK), dtype=bool))
    w = jnp.where(mask[None], w_s, 0)
    mixed = jnp.einsum('gts,bcsgd->bctgd', w, vc) + jnp.swapaxes(b_s, 0, 1)[None, None, :, :, None]
    return u * mixed.reshape(B, S, SGU_WIDTH)


def hybrid_mixer(x, cos, sin, w_in, w_s, b_s, ln_v_g, ln_v_b, w_oa, w_ob, w_out):
    B, S = x.shape[0], x.shape[1]
    proj = x @ w_in
    q, k, v, qi, ki, wi, u, vg, ga, gb = jnp.split(proj, SPLIT_POINTS, axis=-1)
    q = partial_rope(q.reshape(B, S, N_HEADS, HEAD_DIM), cos, sin)
    k = partial_rope(k.reshape(B, S, N_KV_HEADS, HEAD_DIM), cos, sin)
    v = v.reshape(B, S, N_KV_HEADS, HEAD_DIM)
    qi = partial_rope(qi.reshape(B, S, IDX_HEADS, IDX_DIM), cos, sin)
    ki = partial_rope(ki.reshape(B, S, 1, IDX_DIM), cos, sin)[:, :, 0, :]
    y_a = dsa_attention(q, k, v, qi, ki, wi) @ w_oa
    y_b = spatial_gating(jax.nn.gelu(u, approximate=False), jax.nn.gelu(vg, approximate=False),
                         w_s, b_s, ln_v_g, ln_v_b) @ w_ob
    merged = jax.nn.sigmoid(ga) * y_a + jax.nn.sigmoid(gb) * y_b
    return merged @ w_out


def sq_relu_mlp(x, w1, w2):
    return jnp.square(jax.nn.relu(x @ w1)) @ w2


def setup_inputs(seed: int = 0) -> dict:
    key = jax.random.key(seed)
    ks = jax.random.split(key, 20)
    f32 = jnp.float32
    L = DEPTH

    def nrm(k, shape, scale):
        return jax.random.normal(k, shape, f32) * scale

    x = jax.random.normal(ks[0], (BATCH, SEQ, D_MODEL), f32)
    offset = jax.random.randint(ks[1], (BATCH, 1), 0, 1024, dtype=jnp.int32)
    positions = offset + jnp.arange(SEQ, dtype=jnp.int32)[None, :]
    w_in = nrm(ks[2], (L, D_MODEL, N_IN), D_MODEL ** -0.5)
    w_in = w_in.at[:, :, V_START:V_END].multiply(BETA)
    w_s = nrm(ks[3], (L, SGU_GROUPS, SGU_CHUNK, SGU_CHUNK), SGU_CHUNK ** -0.5)
    b_s = 1.0 + nrm(ks[4], (L, SGU_GROUPS, SGU_CHUNK), 0.1)
    ln_v_g = 1.0 + nrm(ks[5], (L, SGU_WIDTH), 0.02)
    ln_v_b = nrm(ks[6], (L, SGU_WIDTH), 0.02)
    w_oa = nrm(ks[7], (L, ATT_Q, D_MODEL), BETA * ATT_Q ** -0.5)
    w_ob = nrm(ks[8], (L, SGU_WIDTH, D_MODEL), BETA * SGU_WIDTH ** -0.5)
    w_out = nrm(ks[9], (L, D_MODEL, D_MODEL), BETA * D_MODEL ** -0.5)
    ln1_g = 1.0 + nrm(ks[10], (L, D_MODEL), 0.02)
    ln1_b = nrm(ks[11], (L, D_MODEL), 0.02)
    w_ff1 = nrm(ks[12], (L, D_MODEL, D_FF), BETA * D_MODEL ** -0.5)
    w_ff2 = nrm(ks[13], (L, D_FF, D_MODEL), BETA * D_FF ** -0.5)
    ln2_g = 1.0 + nrm(ks[14], (L, D_MODEL), 0.02)
    ln2_b = nrm(ks[15], (L, D_MODEL), 0.02)
    return {"x": x, "positions": positions, "w_in": w_in, "w_s": w_s, "b_s": b_s,
            "ln_v_g": ln_v_g, "ln_v_b": ln_v_b, "w_oa": w_oa, "w_ob": w_ob, "w_out": w_out,
            "ln1_g": ln1_g, "ln1_b": ln1_b, "w_ff1": w_ff1, "w_ff2": w_ff2,
            "ln2_g": ln2_g, "ln2_b": ln2_b}


def reference(x, positions, w_in, w_s, b_s, ln_v_g, ln_v_b, w_oa, w_ob, w_out,
              ln1_g, ln1_b, w_ff1, w_ff2, ln2_g, ln2_b):
    cos, sin = rope_tables(positions)
    for l in range(DEPTH):
        mix = hybrid_mixer(x, cos, sin, w_in[l], w_s[l], b_s[l], ln_v_g[l], ln_v_b[l],
                           w_oa[l], w_ob[l], w_out[l])
        x = layer_norm(ALPHA * x + mix, ln1_g[l], ln1_b[l])
        x = layer_norm(ALPHA * x + sq_relu_mlp(x, w_ff1[l], w_ff2[l]), ln2_g[l], ln2_b[l])
    return x
```

```python
import functools

import numpy as np
import jax
import jax.numpy as jnp
from jax import lax
from jax.experimental import pallas as pl
from jax.experimental.pallas import tpu as pltpu

F32 = jnp.float32
BF16 = jnp.bfloat16
I32 = jnp.int32

D_MODEL = 1024
N_HEADS = 8
N_KV_HEADS = 2
HEAD_DIM = 64
ROPE_DIM = HEAD_DIM // 4
ROPE_HALF = ROPE_DIM // 2
ROPE_THETA = 500000.0
IDX_HEADS = 4
IDX_DIM = 64
TOPK_MAX = 256
SGU_CHUNK = 128
SGU_GROUPS = 4
SGU_WIDTH = 512
D_FF = 4 * D_MODEL
LN_EPS = 1e-5

ATT_Q = N_HEADS * HEAD_DIM
ATT_KV = N_KV_HEADS * HEAD_DIM
IDX_Q = IDX_HEADS * IDX_DIM
HEADS_PER_KV = N_HEADS // N_KV_HEADS

LANES = 128
SUBLANES = 8
KEY_CHUNK = LANES
Q_TILE = LANES
COUNT_CHUNKS = 4
VMEM_LIMIT = 56 * 1024 * 1024

ATT_SCALE = HEAD_DIM ** -0.5
IDX_SCALE = (IDX_HEADS ** -0.5) * (IDX_DIM ** -0.5)
INT_MIN = np.int32(-2 ** 31)
NEG_BIG = -1e30

ROW_Q = 0
ROW_K = ROW_Q + ATT_Q
ROW_V = ROW_K + ATT_KV
ROW_QI = ROW_V + ATT_KV
ROW_KI = ROW_QI + IDX_Q
ROW_WI = ROW_KI + LANES
ROWS_T = ROW_WI + 16
COL_U = 0
COL_VG = COL_U + SGU_WIDTH
COL_GA = COL_VG + SGU_WIDTH
COL_GB = COL_GA + D_MODEL
COLS_S = COL_GB + D_MODEL


def _layer_norm(z, g, b):
    mu = jnp.mean(z, axis=-1, keepdims=True)
    zc = z - mu
    var = jnp.mean(zc * zc, axis=-1, keepdims=True)
    return zc * lax.rsqrt(var + LN_EPS) * g + b


def _gelu(z):
    return 0.5 * z * (1.0 + lax.erf(z * np.float32(np.sqrt(0.5))))


def _dot(a, b):
    return jnp.dot(a, b, preferred_element_type=F32)


def _dot_nt(a, b):
    return lax.dot_general(a, b, (((1,), (1,)), ((), ())), preferred_element_type=F32)


def _rope_kernel(pos_ref, inv_ref, cos_ref, sin_ref):
    ang = pos_ref[0].astype(F32) * inv_ref[:, 0:1]
    cos_ref[0] = jnp.cos(ang)
    sin_ref[0] = jnp.sin(ang)


def _rope_tables(positions):
    B, S = positions.shape
    inv_freq = ROPE_THETA ** (-jnp.arange(0, ROPE_DIM, 2, dtype=F32) / ROPE_DIM)
    inv_b = jnp.broadcast_to(inv_freq[:, None], (ROPE_HALF, LANES))
    return pl.pallas_call(
        _rope_kernel,
        grid=(B,),
        in_specs=[pl.BlockSpec((1, 1, S), lambda b: (b, 0, 0)),
                  pl.BlockSpec((ROPE_HALF, LANES), lambda b: (0, 0))],
        out_specs=[pl.BlockSpec((1, ROPE_HALF, S), lambda b: (b, 0, 0)),
                   pl.BlockSpec((1, ROPE_HALF, S), lambda b: (b, 0, 0))],
        out_shape=[jax.ShapeDtypeStruct((B, ROPE_HALF, S), F32)] * 2,
        name="rope_tables",
    )(positions.reshape(B, 1, S), inv_b)


def _rope_rows(t, n_heads, c, s):
    pieces = []
    for h in range(n_heads):
        base = h * HEAD_DIM
        x1 = t[base:base + ROPE_HALF]
        x2 = t[base + ROPE_HALF:base + ROPE_DIM]
        pieces += [x1 * c - x2 * s, x2 * c + x1 * s, t[base + ROPE_DIM:base + HEAD_DIM]]
    return jnp.concatenate(pieces, axis=0)


def _in_proj_kernel(x_ref, cos_ref, sin_ref, wt_ref, ws_ref, wob_ref, wsgu_ref, bs_ref, lng_ref, lnb_ref,
                    qT_ref, k_ref, vT_ref, qiT_ref, ki_ref, wiT_ref, sa_ref, pb_ref, *, tm):
    xb = x_ref[0].astype(BF16)
    c = cos_ref[0]
    s = sin_ref[0]

    def proj_t(lo, hi):
        return _dot_nt(wt_ref[lo:hi, :], xb)

    qT_ref[0] = (_rope_rows(proj_t(ROW_Q, ROW_K), N_HEADS, c, s) * ATT_SCALE).astype(BF16)
    k_ref[0] = _rope_rows(proj_t(ROW_K, ROW_V), N_KV_HEADS, c, s).T.astype(BF16)
    vT = proj_t(ROW_V, ROW_QI).astype(BF16)
    for cc in range(tm // KEY_CHUNK):
        vT_ref[0, cc] = vT[:, cc * KEY_CHUNK:(cc + 1) * KEY_CHUNK]
    qiT_ref[0] = _rope_rows(proj_t(ROW_QI, ROW_KI), IDX_HEADS, c, s).astype(BF16)
    kiT = proj_t(ROW_KI, ROW_WI)
    kiT = jnp.concatenate([_rope_rows(kiT[0:IDX_DIM], 1, c, s), kiT[IDX_DIM:]], axis=0)
    ki_ref[0] = kiT.T.astype(BF16)
    wiT_ref[0] = proj_t(ROW_WI, ROWS_T)[0:SUBLANES] * IDX_SCALE

    u = _gelu(_dot(xb, ws_ref[:, COL_U:COL_VG]))
    vg = _gelu(_dot(xb, ws_ref[:, COL_VG:COL_GA]))
    vn = _layer_norm(vg, lng_ref[...], lnb_ref[...]).astype(BF16)
    row = lax.broadcasted_iota(I32, (SGU_CHUNK, SGU_CHUNK), 0)
    col = lax.broadcasted_iota(I32, (SGU_CHUNK, SGU_CHUNK), 1)
    w_tril = [jnp.where(row >= col, wsgu_ref[g], 0.0).astype(BF16) for g in range(SGU_GROUPS)]
    gd = SGU_WIDTH // SGU_GROUPS
    chunks = []
    for cc in range(tm // SGU_CHUNK):
        r0 = cc * SGU_CHUNK
        cols = [_dot(w_tril[g], vn[r0:r0 + SGU_CHUNK, g * gd:(g + 1) * gd]) for g in range(SGU_GROUPS)]
        chunks.append(jnp.concatenate(cols, axis=1) + bs_ref[...])
    sgu = u * jnp.concatenate(chunks, axis=0)
    yb = _dot(sgu.astype(BF16), wob_ref[...])

    sa_ref[0] = jax.nn.sigmoid(_dot(xb, ws_ref[:, COL_GA:COL_GB])).astype(BF16)
    pb_ref[0] = (jax.nn.sigmoid(_dot(xb, ws_ref[:, COL_GB:COLS_S])) * yb).astype(BF16)


def _in_proj(x, cosT, sinT, wt, ws, wob, wsgu, bs_b, lng, lnb, *, tm):
    B, S, D = x.shape
    nt = S // tm
    const2 = lambda b, t: (0, 0)
    out_shape = [
        jax.ShapeDtypeStruct((B, ATT_Q, S), BF16),
        jax.ShapeDtypeStruct((B, S, ATT_KV), BF16),
        jax.ShapeDtypeStruct((B, S // KEY_CHUNK, ATT_KV, KEY_CHUNK), BF16),
        jax.ShapeDtypeStruct((B, IDX_Q, S), BF16),
        jax.ShapeDtypeStruct((B, S, LANES), BF16),
        jax.ShapeDtypeStruct((B, SUBLANES, S), F32),
        jax.ShapeDtypeStruct((B, S, D), BF16),
        jax.ShapeDtypeStruct((B, S, D), BF16),
    ]
    out_specs = [
        pl.BlockSpec((1, ATT_Q, tm), lambda b, t: (b, 0, t)),
        pl.BlockSpec((1, tm, ATT_KV), lambda b, t: (b, t, 0)),
        pl.BlockSpec((1, tm // KEY_CHUNK, ATT_KV, KEY_CHUNK), lambda b, t: (b, t, 0, 0)),
        pl.BlockSpec((1, IDX_Q, tm), lambda b, t: (b, 0, t)),
        pl.BlockSpec((1, tm, LANES), lambda b, t: (b, t, 0)),
        pl.BlockSpec((1, SUBLANES, tm), lambda b, t: (b, 0, t)),
        pl.BlockSpec((1, tm, D), lambda b, t: (b, t, 0)),
        pl.BlockSpec((1, tm, D), lambda b, t: (b, t, 0)),
    ]
    in_specs = [
        pl.BlockSpec((1, tm, D), lambda b, t: (b, t, 0)),
        pl.BlockSpec((1, ROPE_HALF, tm), lambda b, t: (b, 0, t)),
        pl.BlockSpec((1, ROPE_HALF, tm), lambda b, t: (b, 0, t)),
        pl.BlockSpec((ROWS_T, D), const2),
        pl.BlockSpec((D, COLS_S), const2),
        pl.BlockSpec((SGU_WIDTH, D), const2),
        pl.BlockSpec((SGU_GROUPS, SGU_CHUNK, SGU_CHUNK), lambda b, t: (0, 0, 0)),
        pl.BlockSpec((SGU_CHUNK, SGU_WIDTH), const2),
        pl.BlockSpec((1, SGU_WIDTH), const2),
        pl.BlockSpec((1, SGU_WIDTH), const2),
    ]
    return pl.pallas_call(
        functools.partial(_in_proj_kernel, tm=tm),
        grid=(B, nt),
        in_specs=in_specs,
        out_specs=out_specs,
        out_shape=out_shape,
        compiler_params=pltpu.CompilerParams(
            dimension_semantics=("parallel", "parallel"), vmem_limit_bytes=VMEM_LIMIT),
        name="in_proj",
    )(x, cosT, sinT, wt, ws, wob, wsgu, bs_b, lng, lnb)


def _sum_rows_to_vreg(v):
    parts = [v[r:r + SUBLANES] for r in range(0, v.shape[0], SUBLANES)]
    while len(parts) > 1:
        nxt = [parts[a] + parts[a + 1] for a in range(0, len(parts) - 1, 2)]
        if len(parts) % 2:
            nxt.append(parts[-1])
        parts = nxt
    return parts[0]


def _dsa_kernel(qT_ref, qiT_ref, wiT_ref, k_ref, ki_ref, vT_ref, o_ref,
                keys_sc, rq_sc, ri_sc, acc_sc, m_sc, l_sc, *, topk):
    i = pl.program_id(1)
    n_chunks = i + 1
    n_count = (i + COUNT_CHUNKS) // COUNT_CHUNKS

    rq_sc[...] = jnp.zeros(rq_sc.shape, BF16)
    for h in range(N_HEADS):
        g = h // HEADS_PER_KV
        rq_sc[g * HEAD_DIM:(g + 1) * HEAD_DIM, h * Q_TILE:(h + 1) * Q_TILE] = \
            qT_ref[0, h * HEAD_DIM:(h + 1) * HEAD_DIM, :]
    ri_sc[...] = jnp.zeros(ri_sc.shape, BF16)
    for h in range(IDX_HEADS):
        ri_sc[0:IDX_DIM, h * Q_TILE:(h + 1) * Q_TILE] = qiT_ref[0, h * IDX_DIM:(h + 1) * IDX_DIM, :]

    w = wiT_ref[0]
    row = lax.broadcasted_iota(I32, (KEY_CHUNK, Q_TILE), 0)
    col = lax.broadcasted_iota(I32, (KEY_CHUNK, Q_TILE), 1)
    q_pos = i * Q_TILE + col

    def score_body(j, carry):
        kc = ki_ref[0, pl.ds(pl.multiple_of(j * KEY_CHUNK, KEY_CHUNK), KEY_CHUNK), :]
        lg = _dot(kc, ri_sc[...])
        sc = w[0:1, :] * jnp.maximum(lg[:, 0:Q_TILE], 0.0)
        for h in range(1, IDX_HEADS):
            sc = sc + w[h:h + 1, :] * jnp.maximum(lg[:, h * Q_TILE:(h + 1) * Q_TILE], 0.0)
        bits = pltpu.bitcast(sc, I32)
        key = bits ^ ((bits >> 31) & np.int32(0x7FFFFFFF))
        keys_sc[j] = jnp.where(j * KEY_CHUNK + row <= q_pos, key, INT_MIN)
        return carry

    lax.fori_loop(0, n_count * COUNT_CHUNKS, score_body, 0)

    def count_pass(pred):
        def body(jj, acc):
            for cc in range(COUNT_CHUNKS):
                blk = keys_sc[jj * COUNT_CHUNKS + cc]
                acc = acc + _sum_rows_to_vreg(jnp.where(pred(blk), 1, 0).astype(I32))
            return acc
        acc = lax.fori_loop(0, n_count, body, jnp.zeros((SUBLANES, Q_TILE), I32))
        return jnp.sum(acc, axis=0, keepdims=True)

    def bisect_body(r, t):
        cand = t + lax.shift_left(np.int32(1), 31 - r)
        total = count_pass(lambda blk: blk >= cand)
        return jnp.where(total >= topk, cand, t)

    t = lax.fori_loop(0, 32, bisect_body, jnp.full((1, Q_TILE), INT_MIN, I32))
    n_gt = count_pass(lambda blk: blk > t)
    n_ties = (topk - n_gt).astype(F32)

    m_sc[...] = jnp.full(m_sc.shape, NEG_BIG, F32)
    l_sc[...] = jnp.zeros(l_sc.shape, F32)
    acc_sc[...] = jnp.zeros(acc_sc.shape, F32)
    tri = jnp.where(row >= col, 1.0, 0.0).astype(BF16)

    def attn_body(j, tie_carry):
        key = keys_sc[j]
        eq = key == t
        pref = _dot(tri, jnp.where(eq, 1.0, 0.0).astype(BF16)) + tie_carry
        sel = ((key > t) | (eq & (pref <= n_ties))) & (key != INT_MIN)
        kc = k_ref[0, pl.ds(pl.multiple_of(j * KEY_CHUNK, KEY_CHUNK), KEY_CHUNK), :]
        s = _dot(kc, rq_sc[...])
        vt = vT_ref[0, j]
        for h in range(N_HEADS):
            g = h // HEADS_PER_KV
            cs = slice(h * Q_TILE, (h + 1) * Q_TILE)
            sm = jnp.where(sel, s[:, cs], NEG_BIG)
            m_old = m_sc[:, cs]
            m_new = jnp.maximum(m_old, jnp.max(sm, axis=0, keepdims=True))
            alpha = jnp.exp(m_old - m_new)
            p = jnp.exp(sm - m_new)
            l_sc[:, cs] = alpha * l_sc[:, cs] + jnp.sum(p, axis=0, keepdims=True)
            m_sc[:, cs] = m_new
            pv = _dot(vt[g * HEAD_DIM:(g + 1) * HEAD_DIM, :], p.astype(BF16))
            acc_sc[:, cs] = alpha * acc_sc[:, cs] + pv
        return pref[KEY_CHUNK - 1:KEY_CHUNK, :]

    lax.fori_loop(0, n_chunks, attn_body, jnp.zeros((1, Q_TILE), F32))

    out_t = acc_sc[...] / l_sc[...]
    for pr in range(N_HEADS // 2):
        pair = jnp.concatenate([out_t[:, (2 * pr) * Q_TILE:(2 * pr + 1) * Q_TILE],
                                out_t[:, (2 * pr + 1) * Q_TILE:(2 * pr + 2) * Q_TILE]], axis=0)
        o_ref[0, :, pr * LANES:(pr + 1) * LANES] = pair.T.astype(BF16)


def _dsa(qT, qiT, wiT, k, ki, vT, *, topk):
    B, _, S = qT.shape
    nq = S // Q_TILE
    nk = S // KEY_CHUNK
    return pl.pallas_call(
        functools.partial(_dsa_kernel, topk=topk),
        grid=(B, nq),
        in_specs=[
            pl.BlockSpec((1, ATT_Q, Q_TILE), lambda b, i: (b, 0, i)),
            pl.BlockSpec((1, IDX_Q, Q_TILE), lambda b, i: (b, 0, i)),
            pl.BlockSpec((1, SUBLANES, Q_TILE), lambda b, i: (b, 0, i)),
            pl.BlockSpec((1, S, ATT_KV), lambda b, i: (b, 0, 0)),
            pl.BlockSpec((1, S, LANES), lambda b, i: (b, 0, 0)),
            pl.BlockSpec((1, nk, ATT_KV, KEY_CHUNK), lambda b, i: (b, 0, 0, 0)),
        ],
        out_specs=pl.BlockSpec((1, Q_TILE, ATT_Q), lambda b, i: (b, i, 0)),
        out_shape=jax.ShapeDtypeStruct((B, S, ATT_Q), BF16),
        scratch_shapes=[
            pltpu.VMEM((nk + COUNT_CHUNKS, KEY_CHUNK, Q_TILE), I32),
            pltpu.VMEM((ATT_KV, N_HEADS * Q_TILE), BF16),
            pltpu.VMEM((LANES, IDX_HEADS * Q_TILE), BF16),
            pltpu.VMEM((HEAD_DIM, N_HEADS * Q_TILE), F32),
            pltpu.VMEM((1, N_HEADS * Q_TILE), F32),
            pltpu.VMEM((1, N_HEADS * Q_TILE), F32),
        ],
        compiler_params=pltpu.CompilerParams(
            dimension_semantics=("parallel", "arbitrary"), vmem_limit_bytes=VMEM_LIMIT),
        name="dsa",
    )(qT, qiT, wiT, k, ki, vT)


def _post_kernel(attn_ref, sa_ref, pb_ref, x_ref, woa_ref, wout_ref, g1_ref, b1_ref,
                 w1_ref, w2_ref, g2_ref, b2_ref, o_ref, *, alpha, ff_chunk):
    ya = _dot(attn_ref[0], woa_ref[...])
    merged = sa_ref[0].astype(F32) * ya + pb_ref[0].astype(F32)
    mix = _dot(merged.astype(BF16), wout_ref[...])
    x1 = _layer_norm(alpha * x_ref[0] + mix, g1_ref[...], b1_ref[...])
    xb = x1.astype(BF16)
    acc = jnp.zeros(x1.shape, F32)
    for cc in range(D_FF // ff_chunk):
        h = _dot(xb, w1_ref[:, cc * ff_chunk:(cc + 1) * ff_chunk])
        h = jnp.square(jnp.maximum(h, 0.0)).astype(BF16)
        acc = acc + _dot(h, w2_ref[cc * ff_chunk:(cc + 1) * ff_chunk, :])
    o_ref[0] = _layer_norm(alpha * x1 + acc, g2_ref[...], b2_ref[...])


def _post(attn, sa, pb, x, woa, wout, g1, b1, w1, w2, g2, b2, *, alpha, tm):
    B, S, D = x.shape
    const2 = lambda b, t: (0, 0)
    tok = lambda width: pl.BlockSpec((1, tm, width), lambda b, t: (b, t, 0))
    resident = lambda shape: pl.BlockSpec(shape, const2, pipeline_mode=pl.Buffered(1))
    return pl.pallas_call(
        functools.partial(_post_kernel, alpha=alpha, ff_chunk=512),
        grid=(B, S // tm),
        in_specs=[tok(ATT_Q), tok(D), tok(D), tok(D),
                  resident((ATT_Q, D)), resident((D, D)), resident((1, D)), resident((1, D)),
                  resident((D, D_FF)), resident((D_FF, D)), resident((1, D)), resident((1, D))],
        out_specs=tok(D),
        out_shape=jax.ShapeDtypeStruct((B, S, D), F32),
        compiler_params=pltpu.CompilerParams(
            dimension_semantics=("parallel", "parallel"), vmem_limit_bytes=VMEM_LIMIT),
        name="post",
    )(attn, sa, pb, x, woa, wout, g1, b1, w1, w2, g2, b2)


def _split_in_weights(w):
    n_t = ATT_Q + 2 * ATT_KV + IDX_Q + IDX_DIM
    d = w.shape[0]
    wt = jnp.concatenate([
        w[:, :n_t].T,
        jnp.zeros((LANES - IDX_DIM, d), w.dtype),
        w[:, n_t:n_t + IDX_HEADS].T,
        jnp.zeros((ROWS_T - ROW_WI - IDX_HEADS, d), w.dtype),
    ], axis=0)
    return wt.astype(BF16), w[:, n_t + IDX_HEADS:].astype(BF16)


def kernel(x, positions, w_in, w_s, b_s, ln_v_g, ln_v_b, w_oa, w_ob, w_out,
           ln1_g, ln1_b, w_ff1, w_ff2, ln2_g, ln2_b):
    B, S, D = x.shape
    depth = w_in.shape[0]
    assert D == D_MODEL and S % 512 == 0
    alpha = float((2 * depth) ** 0.25)
    topk = min(TOPK_MAX, S // 4)
    cosT, sinT = _rope_tables(positions)
    row2 = lambda v: v.reshape(1, -1)
    for l in range(depth):
        wt, ws = _split_in_weights(w_in[l])
        bs_b = jnp.repeat(b_s[l].T, SGU_WIDTH // SGU_GROUPS, axis=1)
        qT, k, vT, qiT, ki, wiT, sa, pb = _in_proj(
            x, cosT, sinT, wt, ws, w_ob[l].astype(BF16), w_s[l], bs_b,
            row2(ln_v_g[l]), row2(ln_v_b[l]), tm=256)
        attn = _dsa(qT, qiT, wiT, k, ki, vT, topk=topk)
        x = _post(attn, sa, pb, x, w_oa[l].astype(BF16), w_out[l].astype(BF16),
                  row2(ln1_g[l]), row2(ln1_b[l]), w_ff1[l].astype(BF16), w_ff2[l].astype(BF16),
                  row2(ln2_g[l]), row2(ln2_b[l]), alpha=alpha, tm=512)
    return x
```

```python
import functools

import numpy as np
import jax
import jax.numpy as jnp
from jax import lax
from jax.experimental import pallas as pl
from jax.experimental.pallas import tpu as pltpu

F32 = jnp.float32
BF16 = jnp.bfloat16
I32 = jnp.int32

D_MODEL = 1024
N_HEADS = 8
N_KV_HEADS = 2
HEAD_DIM = 64
ROPE_DIM = HEAD_DIM // 4
ROPE_HALF = ROPE_DIM // 2
ROPE_THETA = 500000.0
IDX_HEADS = 4
IDX_DIM = 64
TOPK_MAX = 256
SGU_CHUNK = 128
SGU_GROUPS = 4
SGU_WIDTH = 512
D_FF = 4 * D_MODEL
LN_EPS = 1e-5

ATT_Q = N_HEADS * HEAD_DIM
ATT_KV = N_KV_HEADS * HEAD_DIM
IDX_Q = IDX_HEADS * IDX_DIM
HEADS_PER_KV = N_HEADS // N_KV_HEADS

LANES = 128
SUBLANES = 8
KEY_CHUNK = LANES
Q_TILE = LANES
GROUP = 4
GROUP_KEYS = GROUP * KEY_CHUNK
IN_PROJ_TOKENS = 512
POST_TOKENS = 512
FF_CHUNK = 512
VMEM_LIMIT = 56 * 1024 * 1024

LOG2E = float(np.log2(np.e))
ATT_SCALE = HEAD_DIM ** -0.5
IDX_SCALE = (IDX_HEADS ** -0.5) * (IDX_DIM ** -0.5)
NEG_BIG = -1e30

KEY_NEG_INF = np.int32(np.array(0xFF800000, np.uint32).view(np.int32) ^ np.int32(0x7FFFFFFF))
KEY_POS_INF = np.int32(0x7F800000)
INT_MIN = np.int32(-2 ** 31)

ROW_Q = 0
ROW_K = ROW_Q + ATT_Q
ROW_V = ROW_K + ATT_KV
ROW_QI = ROW_V + ATT_KV
ROW_KI = ROW_QI + IDX_Q
ROWS_T = ROW_KI + LANES
COL_T_END = ROW_KI + IDX_DIM + IDX_HEADS
COL_U = 0
COL_VG = COL_U + SGU_WIDTH
COL_GA = COL_VG + SGU_WIDTH
COL_GB = COL_GA + D_MODEL
COLS_S = COL_GB + D_MODEL


def _layer_norm(z, g, b):
    mu = jnp.mean(z, axis=-1, keepdims=True)
    zc = z - mu
    var = jnp.mean(zc * zc, axis=-1, keepdims=True)
    return zc * lax.rsqrt(var + LN_EPS) * g + b


def _gelu(z):
    return 0.5 * z * (1.0 + lax.erf(z * np.float32(np.sqrt(0.5))))


def _dot(a, b):
    return jnp.dot(a, b, preferred_element_type=F32)


def _dot_nt(a, b):
    return lax.dot_general(a, b, (((1,), (1,)), ((), ())), preferred_element_type=F32)


def _resident(shape):
    return pl.BlockSpec(shape, lambda *_: (0,) * len(shape), pipeline_mode=pl.Buffered(1))


def _rope_kernel(pos_ref, inv_ref, cos_ref, sin_ref):
    ang = pos_ref[0].astype(F32) * inv_ref[:, 0:1]
    cos_ref[0] = jnp.cos(ang)
    sin_ref[0] = jnp.sin(ang)


def _rope_tables(positions):
    B, S = positions.shape
    inv_freq = ROPE_THETA ** (-jnp.arange(0, ROPE_DIM, 2, dtype=F32) / ROPE_DIM)
    inv_b = jnp.broadcast_to(inv_freq[:, None], (ROPE_HALF, LANES))
    return pl.pallas_call(
        _rope_kernel,
        grid=(B,),
        in_specs=[pl.BlockSpec((1, 1, S), lambda b: (b, 0, 0)),
                  pl.BlockSpec((ROPE_HALF, LANES), lambda b: (0, 0))],
        out_specs=[pl.BlockSpec((1, ROPE_HALF, S), lambda b: (b, 0, 0)),
                   pl.BlockSpec((1, ROPE_HALF, S), lambda b: (b, 0, 0))],
        out_shape=[jax.ShapeDtypeStruct((B, ROPE_HALF, S), F32)] * 2,
        name="rope_tables",
    )(positions.reshape(B, 1, S), inv_b)


def _wt_kernel(w_ref, o_ref):
    o_ref[0] = w_ref[0].T.astype(BF16)


def _transposed_in_weights(w_in):
    depth, d, _ = w_in.shape
    return pl.pallas_call(
        _wt_kernel,
        grid=(depth, ROWS_T // LANES),
        in_specs=[pl.BlockSpec((1, d, LANES), lambda l, c: (l, 0, c))],
        out_specs=pl.BlockSpec((1, LANES, d), lambda l, c: (l, c, 0)),
        out_shape=jax.ShapeDtypeStruct((depth, ROWS_T, d), BF16),
        name="w_in_transpose",
    )(w_in)


def _rope_rows(t, n_heads, c, s):
    pieces = []
    for h in range(n_heads):
        base = h * HEAD_DIM
        x1 = t[base:base + ROPE_HALF]
        x2 = t[base + ROPE_HALF:base + ROPE_DIM]
        pieces += [x1 * c - x2 * s, x2 * c + x1 * s, t[base + ROPE_DIM:base + HEAD_DIM]]
    return jnp.concatenate(pieces, axis=0)


def _in_proj_kernel(x_ref, cos_ref, sin_ref, wt_ref, ws_ref, wob_ref, wsgu_ref, bs_ref, lng_ref, lnb_ref,
                    qT_ref, k_ref, vT_ref, qiT_ref, ki_ref, wiT_ref, sa_ref, pb_ref, *, tm):
    xb = x_ref[0].astype(BF16)
    c = cos_ref[0]
    s = sin_ref[0]

    def proj_t(lo, hi):
        return _dot_nt(wt_ref[lo:hi, :], xb)

    qT_ref[0] = (_rope_rows(proj_t(ROW_Q, ROW_K), N_HEADS, c, s) * (ATT_SCALE * LOG2E)).astype(BF16)
    k_ref[0] = _rope_rows(proj_t(ROW_K, ROW_V), N_KV_HEADS, c, s).T.astype(BF16)
    vT = proj_t(ROW_V, ROW_QI).astype(BF16)
    for cc in range(tm // KEY_CHUNK):
        vT_ref[0, cc] = vT[:, cc * KEY_CHUNK:(cc + 1) * KEY_CHUNK]
    qiT_ref[0] = _rope_rows(proj_t(ROW_QI, ROW_KI), IDX_HEADS, c, s).astype(BF16)
    kiT = proj_t(ROW_KI, ROWS_T)
    wiT_ref[0] = kiT[IDX_DIM:IDX_DIM + SUBLANES] * IDX_SCALE
    kiT = jnp.concatenate([_rope_rows(kiT[0:IDX_DIM], 1, c, s), kiT[IDX_DIM:]], axis=0)
    ki_ref[0] = kiT.T.astype(BF16)

    u = _gelu(_dot(xb, ws_ref[:, COL_U:COL_VG]))
    vg = _gelu(_dot(xb, ws_ref[:, COL_VG:COL_GA]))
    vn = _layer_norm(vg, lng_ref[...], lnb_ref[...]).astype(BF16)
    row = lax.broadcasted_iota(I32, (SGU_CHUNK, SGU_CHUNK), 0)
    col = lax.broadcasted_iota(I32, (SGU_CHUNK, SGU_CHUNK), 1)
    w_tril = [jnp.where(row >= col, wsgu_ref[g], 0.0).astype(BF16) for g in range(SGU_GROUPS)]
    gd = SGU_WIDTH // SGU_GROUPS
    chunks = []
    for cc in range(tm // SGU_CHUNK):
        r0 = cc * SGU_CHUNK
        cols = [_dot(w_tril[g], vn[r0:r0 + SGU_CHUNK, g * gd:(g + 1) * gd]) for g in range(SGU_GROUPS)]
        chunks.append(jnp.concatenate(cols, axis=1) + bs_ref[...])
    sgu = u * jnp.concatenate(chunks, axis=0)
    yb = _dot(sgu.astype(BF16), wob_ref[...])

    sa_ref[0] = jax.nn.sigmoid(_dot(xb, ws_ref[:, COL_GA:COL_GB])).astype(BF16)
    pb_ref[0] = (jax.nn.sigmoid(_dot(xb, ws_ref[:, COL_GB:COLS_S])) * yb).astype(BF16)


def _in_proj(x, cosT, sinT, wt, ws, wob, wsgu, bs_b, lng, lnb):
    B, S, D = x.shape
    tm = IN_PROJ_TOKENS
    out_shape = [
        jax.ShapeDtypeStruct((B, ATT_Q, S), BF16),
        jax.ShapeDtypeStruct((B, S, ATT_KV), BF16),
        jax.ShapeDtypeStruct((B, S // KEY_CHUNK, ATT_KV, KEY_CHUNK), BF16),
        jax.ShapeDtypeStruct((B, IDX_Q, S), BF16),
        jax.ShapeDtypeStruct((B, S, LANES), BF16),
        jax.ShapeDtypeStruct((B, SUBLANES, S), F32),
        jax.ShapeDtypeStruct((B, S, D), BF16),
        jax.ShapeDtypeStruct((B, S, D), BF16),
    ]
    out_specs = [
        pl.BlockSpec((1, ATT_Q, tm), lambda b, t: (b, 0, t)),
        pl.BlockSpec((1, tm, ATT_KV), lambda b, t: (b, t, 0)),
        pl.BlockSpec((1, tm // KEY_CHUNK, ATT_KV, KEY_CHUNK), lambda b, t: (b, t, 0, 0)),
        pl.BlockSpec((1, IDX_Q, tm), lambda b, t: (b, 0, t)),
        pl.BlockSpec((1, tm, LANES), lambda b, t: (b, t, 0)),
        pl.BlockSpec((1, SUBLANES, tm), lambda b, t: (b, 0, t)),
        pl.BlockSpec((1, tm, D), lambda b, t: (b, t, 0)),
        pl.BlockSpec((1, tm, D), lambda b, t: (b, t, 0)),
    ]
    in_specs = [
        pl.BlockSpec((1, tm, D), lambda b, t: (b, t, 0)),
        pl.BlockSpec((1, ROPE_HALF, tm), lambda b, t: (b, 0, t)),
        pl.BlockSpec((1, ROPE_HALF, tm), lambda b, t: (b, 0, t)),
        _resident((ROWS_T, D)),
        _resident((D, COLS_S)),
        _resident((SGU_WIDTH, D)),
        _resident((SGU_GROUPS, SGU_CHUNK, SGU_CHUNK)),
        _resident((SGU_CHUNK, SGU_WIDTH)),
        _resident((1, SGU_WIDTH)),
        _resident((1, SGU_WIDTH)),
    ]
    return pl.pallas_call(
        functools.partial(_in_proj_kernel, tm=tm),
        grid=(B, S // tm),
        in_specs=in_specs,
        out_specs=out_specs,
        out_shape=out_shape,
        compiler_params=pltpu.CompilerParams(
            dimension_semantics=("parallel", "parallel"), vmem_limit_bytes=VMEM_LIMIT),
        name="in_proj",
    )(x, cosT, sinT, wt, ws, wob, wsgu, bs_b, lng, lnb)


def _sum_rows_to_vreg(v):
    parts = [v[r:r + SUBLANES] for r in range(0, v.shape[0], SUBLANES)]
    while len(parts) > 1:
        nxt = [parts[a] + parts[a + 1] for a in range(0, len(parts) - 1, 2)]
        if len(parts) % 2:
            nxt.append(parts[-1])
        parts = nxt
    return parts[0]


def _key_to_float(key):
    key = jnp.clip(key, KEY_NEG_INF, KEY_POS_INF)
    return pltpu.bitcast(key ^ ((key >> 31) & np.int32(0x7FFFFFFF)), F32)


def _dsa_kernel(qT_ref, qiT_ref, wiT_ref, k_ref, ki_ref, vT_ref, o_ref,
                sc_sc, rq_sc, ri_sc, acc_sc, m_sc, l_sc, *, topk):
    i = pl.program_id(1)
    n_groups = (i + GROUP) // GROUP

    rq_sc[...] = jnp.zeros(rq_sc.shape, BF16)
    for h in range(N_HEADS):
        g = h // HEADS_PER_KV
        rq_sc[g * HEAD_DIM:(g + 1) * HEAD_DIM, h * Q_TILE:(h + 1) * Q_TILE] = \
            qT_ref[0, h * HEAD_DIM:(h + 1) * HEAD_DIM, :]
    ri_sc[...] = jnp.zeros(ri_sc.shape, BF16)
    for h in range(IDX_HEADS):
        ri_sc[0:IDX_DIM, h * Q_TILE:(h + 1) * Q_TILE] = qiT_ref[0, h * IDX_DIM:(h + 1) * IDX_DIM, :]

    w = wiT_ref[0]
    row = lax.broadcasted_iota(I32, (GROUP_KEYS, Q_TILE), 0)
    col = lax.broadcasted_iota(I32, (GROUP_KEYS, Q_TILE), 1)
    q_pos = i * Q_TILE + col

    def group_rows(jj):
        return pl.ds(pl.multiple_of(jj * GROUP_KEYS, GROUP_KEYS), GROUP_KEYS)

    def store_group(jj, v):
        for cc in range(GROUP):
            sc_sc[jj * GROUP + cc] = v[cc * KEY_CHUNK:(cc + 1) * KEY_CHUNK]

    def load_group(jj):
        return jnp.concatenate([sc_sc[jj * GROUP + cc] for cc in range(GROUP)], axis=0)

    def score_body(jj, carry):
        lg = _dot(ki_ref[0, group_rows(jj), :], ri_sc[...])
        sc = w[0:1, :] * jnp.maximum(lg[:, 0:Q_TILE], 0.0)
        for h in range(1, IDX_HEADS):
            sc = sc + w[h:h + 1, :] * jnp.maximum(lg[:, h * Q_TILE:(h + 1) * Q_TILE], 0.0)
        store_group(jj, jnp.where(jj * GROUP_KEYS + row <= q_pos, sc, -jnp.inf))
        return carry

    lax.fori_loop(0, n_groups, score_body, 0)

    def count_pass(pred):
        def body(jj, acc):
            for cc in range(GROUP):
                blk = sc_sc[jj * GROUP + cc]
                acc = acc + _sum_rows_to_vreg(jnp.where(pred(blk), 1, 0).astype(I32))
            return acc
        acc = lax.fori_loop(0, n_groups, body, jnp.zeros((SUBLANES, Q_TILE), I32))
        return jnp.sum(acc, axis=0, keepdims=True)

    def bisect_body(r, t_key):
        cand = t_key + lax.shift_left(np.int32(1), 31 - r)
        cand_f = _key_to_float(cand)
        total = count_pass(lambda blk: blk >= cand_f)
        return jnp.where(total >= topk, cand, t_key)

    t_key = lax.fori_loop(0, 32, bisect_body, jnp.full((1, Q_TILE), INT_MIN, I32))
    t = _key_to_float(t_key)
    n_gt = count_pass(lambda blk: blk > t)
    n_ties = (topk - n_gt).astype(F32)

    row_c = lax.broadcasted_iota(I32, (KEY_CHUNK, Q_TILE), 0)
    col_c = lax.broadcasted_iota(I32, (KEY_CHUNK, Q_TILE), 1)
    tri = jnp.where(row_c >= col_c, 1.0, 0.0).astype(BF16)

    def mask_body(jj, tie_carry):
        for cc in range(GROUP):
            sc = sc_sc[jj * GROUP + cc]
            eq = sc == t
            pref = _dot(tri, jnp.where(eq, 1.0, 0.0).astype(BF16)) + tie_carry
            tie_carry = pref[KEY_CHUNK - 1:KEY_CHUNK, :]
            causal = (jj * GROUP + cc) * KEY_CHUNK + row_c <= i * Q_TILE + col_c
            sel = ((sc > t) | (eq & (pref <= n_ties))) & causal
            sc_sc[jj * GROUP + cc] = jnp.where(sel, 0.0, NEG_BIG)
        return tie_carry

    lax.fori_loop(0, n_groups, mask_body, jnp.zeros((1, Q_TILE), F32))

    m_sc[...] = jnp.full(m_sc.shape, NEG_BIG, F32)
    l_sc[...] = jnp.zeros(l_sc.shape, F32)
    acc_sc[...] = jnp.zeros(acc_sc.shape, F32)
    kv_cols = HEADS_PER_KV * Q_TILE

    def attn_body(jj, carry):
        bias = load_group(jj)
        s = _dot(k_ref[0, group_rows(jj), :], rq_sc[...])
        for g in range(N_KV_HEADS):
            ps, alphas = [], []
            for h in range(g * HEADS_PER_KV, (g + 1) * HEADS_PER_KV):
                cs = slice(h * Q_TILE, (h + 1) * Q_TILE)
                sm = s[:, cs] + bias
                m_old = m_sc[:, cs]
                m_new = jnp.maximum(m_old, jnp.max(sm, axis=0, keepdims=True))
                alpha = jnp.exp2(m_old - m_new)
                p = jnp.exp2(sm - m_new)
                l_sc[:, cs] = alpha * l_sc[:, cs] + jnp.sum(p, axis=0, keepdims=True)
                m_sc[:, cs] = m_new
                ps.append(p.astype(BF16))
                alphas.append(alpha)
            gs = slice(g * kv_cols, (g + 1) * kv_cols)
            vt = jnp.concatenate([vT_ref[0, jj * GROUP + cc, g * HEAD_DIM:(g + 1) * HEAD_DIM, :]
                                  for cc in range(GROUP)], axis=1)
            pv = _dot(vt, jnp.concatenate(ps, axis=1))
            acc_sc[:, gs] = jnp.concatenate(alphas, axis=1) * acc_sc[:, gs] + pv
        return carry

    lax.fori_loop(0, n_groups, attn_body, 0)

    out_t = acc_sc[...] / l_sc[...]
    for pr in range(N_HEADS // 2):
        pair = jnp.concatenate([out_t[:, (2 * pr) * Q_TILE:(2 * pr + 1) * Q_TILE],
                                out_t[:, (2 * pr + 1) * Q_TILE:(2 * pr + 2) * Q_TILE]], axis=0)
        o_ref[0, :, pr * LANES:(pr + 1) * LANES] = pair.T.astype(BF16)


def _dsa(qT, qiT, wiT, k, ki, vT, *, topk):
    B, _, S = qT.shape
    nk = S // KEY_CHUNK
    return pl.pallas_call(
        functools.partial(_dsa_kernel, topk=topk),
        grid=(B, S // Q_TILE),
        in_specs=[
            pl.BlockSpec((1, ATT_Q, Q_TILE), lambda b, i: (b, 0, i)),
            pl.BlockSpec((1, IDX_Q, Q_TILE), lambda b, i: (b, 0, i)),
            pl.BlockSpec((1, SUBLANES, Q_TILE), lambda b, i: (b, 0, i)),
            pl.BlockSpec((1, S, ATT_KV), lambda b, i: (b, 0, 0)),
            pl.BlockSpec((1, S, LANES), lambda b, i: (b, 0, 0)),
            pl.BlockSpec((1, nk, ATT_KV, KEY_CHUNK), lambda b, i: (b, 0, 0, 0)),
        ],
        out_specs=pl.BlockSpec((1, Q_TILE, ATT_Q), lambda b, i: (b, i, 0)),
        out_shape=jax.ShapeDtypeStruct((B, S, ATT_Q), BF16),
        scratch_shapes=[
            pltpu.VMEM((nk, KEY_CHUNK, Q_TILE), F32),
            pltpu.VMEM((ATT_KV, N_HEADS * Q_TILE), BF16),
            pltpu.VMEM((LANES, IDX_HEADS * Q_TILE), BF16),
            pltpu.VMEM((HEAD_DIM, N_HEADS * Q_TILE), F32),
            pltpu.VMEM((1, N_HEADS * Q_TILE), F32),
            pltpu.VMEM((1, N_HEADS * Q_TILE), F32),
        ],
        compiler_params=pltpu.CompilerParams(
            dimension_semantics=("parallel", "arbitrary"), vmem_limit_bytes=VMEM_LIMIT),
        name="dsa",
    )(qT, qiT, wiT, k, ki, vT)


def _post_kernel(attn_ref, sa_ref, pb_ref, x_ref, woa_ref, wout_ref, g1_ref, b1_ref,
                 w1_ref, w2_ref, g2_ref, b2_ref, o_ref, *, alpha):
    ya = _dot(attn_ref[0], woa_ref[...])
    merged = sa_ref[0].astype(F32) * ya + pb_ref[0].astype(F32)
    mix = _dot(merged.astype(BF16), wout_ref[...])
    x1 = _layer_norm(alpha * x_ref[0] + mix, g1_ref[...], b1_ref[...])
    xb = x1.astype(BF16)
    acc = jnp.zeros(x1.shape, F32)
    for cc in range(D_FF // FF_CHUNK):
        h = _dot(xb, w1_ref[:, cc * FF_CHUNK:(cc + 1) * FF_CHUNK])
        h = jnp.square(jnp.maximum(h, 0.0)).astype(BF16)
        acc = acc + _dot(h, w2_ref[cc * FF_CHUNK:(cc + 1) * FF_CHUNK, :])
    o_ref[0] = _layer_norm(alpha * x1 + acc, g2_ref[...], b2_ref[...])


def _post(attn, sa, pb, x, woa, wout, g1, b1, w1, w2, g2, b2, *, alpha):
    B, S, D = x.shape
    tm = POST_TOKENS
    tok = lambda width: pl.BlockSpec((1, tm, width), lambda b, t: (b, t, 0))
    return pl.pallas_call(
        functools.partial(_post_kernel, alpha=alpha),
        grid=(B, S // tm),
        in_specs=[tok(ATT_Q), tok(D), tok(D), tok(D),
                  _resident((ATT_Q, D)), _resident((D, D)), _resident((1, D)), _resident((1, D)),
                  _resident((D, D_FF)), _resident((D_FF, D)), _resident((1, D)), _resident((1, D))],
        out_specs=tok(D),
        out_shape=jax.ShapeDtypeStruct((B, S, D), F32),
        compiler_params=pltpu.CompilerParams(
            dimension_semantics=("parallel", "parallel"), vmem_limit_bytes=VMEM_LIMIT),
        name="post",
    )(attn, sa, pb, x, woa, wout, g1, b1, w1, w2, g2, b2)


def kernel(x, positions, w_in, w_s, b_s, ln_v_g, ln_v_b, w_oa, w_ob, w_out,
           ln1_g, ln1_b, w_ff1, w_ff2, ln2_g, ln2_b):
    B, S, D = x.shape
    depth = w_in.shape[0]
    assert D == D_MODEL and S % max(IN_PROJ_TOKENS, POST_TOKENS, GROUP_KEYS) == 0
    assert w_in.shape[2] == COL_T_END + COLS_S
    alpha = float((2 * depth) ** 0.25)
    topk = min(TOPK_MAX, S // 4)
    cosT, sinT = _rope_tables(positions)
    wt_all = _transposed_in_weights(w_in)
    row2 = lambda v: v.reshape(1, -1)
    for l in range(depth):
        ws = w_in[l, :, COL_T_END:].astype(BF16)
        bs_b = jnp.repeat(b_s[l].T, SGU_WIDTH // SGU_GROUPS, axis=1)
        qT, k, vT, qiT, ki, wiT, sa, pb = _in_proj(
            x, cosT, sinT, wt_all[l], ws, w_ob[l].astype(BF16), w_s[l], bs_b,
            row2(ln_v_g[l]), row2(ln_v_b[l]))
        attn = _dsa(qT, qiT, wiT, k, ki, vT, topk=topk)
        x = _post(attn, sa, pb, x, w_oa[l].astype(BF16), w_out[l].astype(BF16),
                  row2(ln1_g[l]), row2(ln1_b[l]), w_ff1[l].astype(BF16), w_ff2[l].astype(BF16),
                  row2(ln2_g[l]), row2(ln2_b[l]), alpha=alpha)
    return x
```

```python
import functools

import numpy as np
import jax
import jax.numpy as jnp
from jax import lax
from jax.experimental import pallas as pl
from jax.experimental.pallas import tpu as pltpu

F32 = jnp.float32
BF16 = jnp.bfloat16
I32 = jnp.int32

D_MODEL = 1024
N_HEADS = 8
N_KV_HEADS = 2
HEAD_DIM = 64
ROPE_DIM = HEAD_DIM // 4
ROPE_HALF = ROPE_DIM // 2
ROPE_THETA = 500000.0
IDX_HEADS = 4
IDX_DIM = 64
TOPK_MAX = 256
SGU_CHUNK = 128
SGU_GROUPS = 4
SGU_WIDTH = 512
D_FF = 4 * D_MODEL
LN_EPS = 1e-5

ATT_Q = N_HEADS * HEAD_DIM
ATT_KV = N_KV_HEADS * HEAD_DIM
IDX_Q = IDX_HEADS * IDX_DIM
HEADS_PER_KV = N_HEADS // N_KV_HEADS

LANES = 128
SUBLANES = 8
KEY_CHUNK = LANES
Q_TILE = 2 * LANES
GROUP = 4
GROUP_KEYS = GROUP * KEY_CHUNK
IN_PROJ_TOKENS = 512
POST_TOKENS = 512
FF_CHUNK = 512
VMEM_LIMIT = 56 * 1024 * 1024

LOG2E = float(np.log2(np.e))
ATT_SCALE = HEAD_DIM ** -0.5
IDX_SCALE = (IDX_HEADS ** -0.5) * (IDX_DIM ** -0.5)
NEG_BIG = -1e30

KEY_NEG_INF = np.int32(np.array(0xFF800000, np.uint32).view(np.int32) ^ np.int32(0x7FFFFFFF))
KEY_POS_INF = np.int32(0x7F800000)
INT_MIN = np.int32(-2 ** 31)

ROW_Q = 0
ROW_K = ROW_Q + ATT_Q
ROW_V = ROW_K + ATT_KV
ROW_QI = ROW_V + ATT_KV
ROW_KI = ROW_QI + IDX_Q
ROWS_T = ROW_KI + LANES
COL_T_END = ROW_KI + IDX_DIM + IDX_HEADS
COL_U = 0
COL_VG = COL_U + SGU_WIDTH
COL_GA = COL_VG + SGU_WIDTH
COL_GB = COL_GA + D_MODEL
COLS_S = COL_GB + D_MODEL


def _layer_norm(z, g, b):
    mu = jnp.mean(z, axis=-1, keepdims=True)
    zc = z - mu
    var = jnp.mean(zc * zc, axis=-1, keepdims=True)
    return zc * lax.rsqrt(var + LN_EPS) * g + b


def _gelu(z):
    return 0.5 * z * (1.0 + lax.erf(z * np.float32(np.sqrt(0.5))))


def _dot(a, b):
    return jnp.dot(a, b, preferred_element_type=F32)


def _dot_nt(a, b):
    return lax.dot_general(a, b, (((1,), (1,)), ((), ())), preferred_element_type=F32)


def _resident(shape):
    return pl.BlockSpec(shape, lambda *_: (0,) * len(shape), pipeline_mode=pl.Buffered(1))


def _rope_kernel(pos_ref, inv_ref, cos_ref, sin_ref):
    ang = pos_ref[0].astype(F32) * inv_ref[:, 0:1]
    cos_ref[0] = jnp.cos(ang)
    sin_ref[0] = jnp.sin(ang)


def _rope_tables(positions):
    B, S = positions.shape
    inv_freq = ROPE_THETA ** (-jnp.arange(0, ROPE_DIM, 2, dtype=F32) / ROPE_DIM)
    inv_b = jnp.broadcast_to(inv_freq[:, None], (ROPE_HALF, LANES))
    return pl.pallas_call(
        _rope_kernel,
        grid=(B,),
        in_specs=[pl.BlockSpec((1, 1, S), lambda b: (b, 0, 0)),
                  pl.BlockSpec((ROPE_HALF, LANES), lambda b: (0, 0))],
        out_specs=[pl.BlockSpec((1, ROPE_HALF, S), lambda b: (b, 0, 0)),
                   pl.BlockSpec((1, ROPE_HALF, S), lambda b: (b, 0, 0))],
        out_shape=[jax.ShapeDtypeStruct((B, ROPE_HALF, S), F32)] * 2,
        name="rope_tables",
    )(positions.reshape(B, 1, S), inv_b)


def _wt_kernel(w_ref, o_ref):
    o_ref[0] = w_ref[0].T.astype(BF16)


def _transposed_in_weights(w_in):
    depth, d, _ = w_in.shape
    return pl.pallas_call(
        _wt_kernel,
        grid=(depth, ROWS_T // LANES),
        in_specs=[pl.BlockSpec((1, d, LANES), lambda l, c: (l, 0, c))],
        out_specs=pl.BlockSpec((1, LANES, d), lambda l, c: (l, c, 0)),
        out_shape=jax.ShapeDtypeStruct((depth, ROWS_T, d), BF16),
        name="w_in_transpose",
    )(w_in)


def _rope_rows(t, n_heads, c, s):
    pieces = []
    for h in range(n_heads):
        base = h * HEAD_DIM
        x1 = t[base:base + ROPE_HALF]
        x2 = t[base + ROPE_HALF:base + ROPE_DIM]
        pieces += [x1 * c - x2 * s, x2 * c + x1 * s, t[base + ROPE_DIM:base + HEAD_DIM]]
    return jnp.concatenate(pieces, axis=0)


def _in_proj_kernel(x_ref, cos_ref, sin_ref, wt_ref, ws_ref, wob_ref, wsgu_ref, bs_ref, lng_ref, lnb_ref,
                    qT_ref, k_ref, vT_ref, qiT_ref, ki_ref, wiT_ref, sa_ref, pb_ref, *, tm):
    xb = x_ref[0].astype(BF16)
    c = cos_ref[0]
    s = sin_ref[0]

    def proj_t(lo, hi):
        return _dot_nt(wt_ref[lo:hi, :], xb)

    qT_ref[0] = (_rope_rows(proj_t(ROW_Q, ROW_K), N_HEADS, c, s) * (ATT_SCALE * LOG2E)).astype(BF16)
    k_ref[0] = _rope_rows(proj_t(ROW_K, ROW_V), N_KV_HEADS, c, s).T.astype(BF16)
    vT = proj_t(ROW_V, ROW_QI).astype(BF16)
    for cc in range(tm // KEY_CHUNK):
        vT_ref[0, cc] = vT[:, cc * KEY_CHUNK:(cc + 1) * KEY_CHUNK]
    qiT_ref[0] = _rope_rows(proj_t(ROW_QI, ROW_KI), IDX_HEADS, c, s).astype(BF16)
    kiT = proj_t(ROW_KI, ROWS_T)
    wiT_ref[0] = kiT[IDX_DIM:IDX_DIM + SUBLANES] * IDX_SCALE
    kiT = jnp.concatenate([_rope_rows(kiT[0:IDX_DIM], 1, c, s), kiT[IDX_DIM:]], axis=0)
    ki_ref[0] = kiT.T.astype(BF16)

    u = _gelu(_dot(xb, ws_ref[:, COL_U:COL_VG]))
    vg = _gelu(_dot(xb, ws_ref[:, COL_VG:COL_GA]))
    vn = _layer_norm(vg, lng_ref[...], lnb_ref[...]).astype(BF16)
    row = lax.broadcasted_iota(I32, (SGU_CHUNK, SGU_CHUNK), 0)
    col = lax.broadcasted_iota(I32, (SGU_CHUNK, SGU_CHUNK), 1)
    w_tril = [jnp.where(row >= col, wsgu_ref[g], 0.0).astype(BF16) for g in range(SGU_GROUPS)]
    gd = SGU_WIDTH // SGU_GROUPS
    chunks = []
    for cc in range(tm // SGU_CHUNK):
        r0 = cc * SGU_CHUNK
        cols = [_dot(w_tril[g], vn[r0:r0 + SGU_CHUNK, g * gd:(g + 1) * gd]) for g in range(SGU_GROUPS)]
        chunks.append(jnp.concatenate(cols, axis=1) + bs_ref[...])
    sgu = u * jnp.concatenate(chunks, axis=0)
    yb = _dot(sgu.astype(BF16), wob_ref[...])

    sa_ref[0] = jax.nn.sigmoid(_dot(xb, ws_ref[:, COL_GA:COL_GB])).astype(BF16)
    pb_ref[0] = (jax.nn.sigmoid(_dot(xb, ws_ref[:, COL_GB:COLS_S])) * yb).astype(BF16)


def _in_proj(x, cosT, sinT, wt, ws, wob, wsgu, bs_b, lng, lnb):
    B, S, D = x.shape
    tm = IN_PROJ_TOKENS
    out_shape = [
        jax.ShapeDtypeStruct((B, ATT_Q, S), BF16),
        jax.ShapeDtypeStruct((B, S, ATT_KV), BF16),
        jax.ShapeDtypeStruct((B, S // KEY_CHUNK, ATT_KV, KEY_CHUNK), BF16),
        jax.ShapeDtypeStruct((B, IDX_Q, S), BF16),
        jax.ShapeDtypeStruct((B, S, LANES), BF16),
        jax.ShapeDtypeStruct((B, SUBLANES, S), F32),
        jax.ShapeDtypeStruct((B, S, D), BF16),
        jax.ShapeDtypeStruct((B, S, D), BF16),
    ]
    out_specs = [
        pl.BlockSpec((1, ATT_Q, tm), lambda b, t: (b, 0, t)),
        pl.BlockSpec((1, tm, ATT_KV), lambda b, t: (b, t, 0)),
        pl.BlockSpec((1, tm // KEY_CHUNK, ATT_KV, KEY_CHUNK), lambda b, t: (b, t, 0, 0)),
        pl.BlockSpec((1, IDX_Q, tm), lambda b, t: (b, 0, t)),
        pl.BlockSpec((1, tm, LANES), lambda b, t: (b, t, 0)),
        pl.BlockSpec((1, SUBLANES, tm), lambda b, t: (b, 0, t)),
        pl.BlockSpec((1, tm, D), lambda b, t: (b, t, 0)),
        pl.BlockSpec((1, tm, D), lambda b, t: (b, t, 0)),
    ]
    in_specs = [
        pl.BlockSpec((1, tm, D), lambda b, t: (b, t, 0)),
        pl.BlockSpec((1, ROPE_HALF, tm), lambda b, t: (b, 0, t)),
        pl.BlockSpec((1, ROPE_HALF, tm), lambda b, t: (b, 0, t)),
        _resident((ROWS_T, D)),
        _resident((D, COLS_S)),
        _resident((SGU_WIDTH, D)),
        _resident((SGU_GROUPS, SGU_CHUNK, SGU_CHUNK)),
        _resident((SGU_CHUNK, SGU_WIDTH)),
        _resident((1, SGU_WIDTH)),
        _resident((1, SGU_WIDTH)),
    ]
    return pl.pallas_call(
        functools.partial(_in_proj_kernel, tm=tm),
        grid=(B, S // tm),
        in_specs=in_specs,
        out_specs=out_specs,
        out_shape=out_shape,
        compiler_params=pltpu.CompilerParams(
            dimension_semantics=("parallel", "parallel"), vmem_limit_bytes=VMEM_LIMIT),
        name="in_proj",
    )(x, cosT, sinT, wt, ws, wob, wsgu, bs_b, lng, lnb)


def _sum_rows_to_vreg(v):
    parts = [v[r:r + SUBLANES] for r in range(0, v.shape[0], SUBLANES)]
    while len(parts) > 1:
        nxt = [parts[a] + parts[a + 1] for a in range(0, len(parts) - 1, 2)]
        if len(parts) % 2:
            nxt.append(parts[-1])
        parts = nxt
    return parts[0]


def _key_to_float(key):
    key = jnp.clip(key, KEY_NEG_INF, KEY_POS_INF)
    return pltpu.bitcast(key ^ ((key >> 31) & np.int32(0x7FFFFFFF)), F32)


def _dsa_kernel(qT_ref, qiT_ref, wiT_ref, k_ref, ki_ref, vT_ref, o_ref,
                sc_sc, rq_sc, ri_sc, acc_sc, m_sc, l_sc, *, topk):
    i = pl.program_id(1)
    n_groups = ((i + 1) * Q_TILE + GROUP_KEYS - 1) // GROUP_KEYS

    rq_sc[...] = jnp.zeros(rq_sc.shape, BF16)
    for h in range(N_HEADS):
        g = h // HEADS_PER_KV
        rq_sc[g * HEAD_DIM:(g + 1) * HEAD_DIM, h * Q_TILE:(h + 1) * Q_TILE] = \
            qT_ref[0, h * HEAD_DIM:(h + 1) * HEAD_DIM, :]
    ri_sc[...] = jnp.zeros(ri_sc.shape, BF16)
    for h in range(IDX_HEADS):
        ri_sc[0:IDX_DIM, h * Q_TILE:(h + 1) * Q_TILE] = qiT_ref[0, h * IDX_DIM:(h + 1) * IDX_DIM, :]

    w = wiT_ref[0]
    row_c = lax.broadcasted_iota(I32, (KEY_CHUNK, Q_TILE), 0)
    col_c = lax.broadcasted_iota(I32, (KEY_CHUNK, Q_TILE), 1)

    def group_rows(jj):
        return pl.ds(pl.multiple_of(jj * GROUP_KEYS, GROUP_KEYS), GROUP_KEYS)

    def load_group(jj):
        return jnp.concatenate([sc_sc[jj * GROUP + cc] for cc in range(GROUP)], axis=0)

    def score_body(jj, carry):
        for cc in range(GROUP):
            ch = jj * GROUP + cc
            kc = ki_ref[0, pl.ds(pl.multiple_of(ch * KEY_CHUNK, KEY_CHUNK), KEY_CHUNK), :]
            lg = _dot(kc, ri_sc[...])
            sc = w[0:1, :] * jnp.maximum(lg[:, 0:Q_TILE], 0.0)
            for h in range(1, IDX_HEADS):
                sc = sc + w[h:h + 1, :] * jnp.maximum(lg[:, h * Q_TILE:(h + 1) * Q_TILE], 0.0)
            sc_sc[ch] = jnp.where(ch * KEY_CHUNK + row_c <= i * Q_TILE + col_c, sc, -jnp.inf)
        return carry

    lax.fori_loop(0, n_groups, score_body, 0)

    def count_pass(pred):
        def body(jj, acc):
            for cc in range(GROUP):
                blk = sc_sc[jj * GROUP + cc]
                acc = acc + _sum_rows_to_vreg(jnp.where(pred(blk), 1, 0).astype(I32))
            return acc
        acc = lax.fori_loop(0, n_groups, body, jnp.zeros((SUBLANES, Q_TILE), I32))
        return jnp.sum(acc, axis=0, keepdims=True)

    def bisect_body(r, t_key):
        cand = t_key + lax.shift_left(np.int32(1), 31 - r)
        cand_f = _key_to_float(cand)
        total = count_pass(lambda blk: blk >= cand_f)
        return jnp.where(total >= topk, cand, t_key)

    t_key = lax.fori_loop(0, 32, bisect_body, jnp.full((1, Q_TILE), INT_MIN, I32))
    t = _key_to_float(t_key)
    n_gt = count_pass(lambda blk: blk > t)
    n_ties = (topk - n_gt).astype(F32)

    tri = jnp.where(lax.broadcasted_iota(I32, (KEY_CHUNK, KEY_CHUNK), 0)
                    >= lax.broadcasted_iota(I32, (KEY_CHUNK, KEY_CHUNK), 1), 1.0, 0.0).astype(BF16)

    def mask_body(jj, tie_carry):
        for cc in range(GROUP):
            ch = jj * GROUP + cc
            sc = sc_sc[ch]
            eq = sc == t
            pref = _dot(tri, jnp.where(eq, 1.0, 0.0).astype(BF16)) + tie_carry
            tie_carry = pref[KEY_CHUNK - 1:KEY_CHUNK, :]
            sel = ((sc > t) | (eq & (pref <= n_ties))) & (ch * KEY_CHUNK + row_c <= i * Q_TILE + col_c)
            sc_sc[ch] = jnp.where(sel, 0.0, NEG_BIG)
        return tie_carry

    lax.fori_loop(0, n_groups, mask_body, jnp.zeros((1, Q_TILE), F32))

    m_sc[...] = jnp.full(m_sc.shape, NEG_BIG, F32)
    l_sc[...] = jnp.zeros(l_sc.shape, F32)
    acc_sc[...] = jnp.zeros(acc_sc.shape, F32)
    kv_cols = HEADS_PER_KV * Q_TILE

    def attn_body(jj, carry):
        bias = load_group(jj)
        s = _dot(k_ref[0, group_rows(jj), :], rq_sc[...])
        for g in range(N_KV_HEADS):
            ps, alphas = [], []
            for h in range(g * HEADS_PER_KV, (g + 1) * HEADS_PER_KV):
                cs = slice(h * Q_TILE, (h + 1) * Q_TILE)
                sm = s[:, cs] + bias
                m_old = m_sc[:, cs]
                m_new = jnp.maximum(m_old, jnp.max(sm, axis=0, keepdims=True))
                alpha = jnp.exp2(m_old - m_new)
                p = jnp.exp2(sm - m_new)
                l_sc[:, cs] = alpha * l_sc[:, cs] + jnp.sum(p, axis=0, keepdims=True)
                m_sc[:, cs] = m_new
                ps.append(p.astype(BF16))
                alphas.append(alpha)
            gs = slice(g * kv_cols, (g + 1) * kv_cols)
            vt = jnp.concatenate([vT_ref[0, jj * GROUP + cc, g * HEAD_DIM:(g + 1) * HEAD_DIM, :]
                                  for cc in range(GROUP)], axis=1)
            pv = _dot(vt, jnp.concatenate(ps, axis=1))
            acc_sc[:, gs] = jnp.concatenate(alphas, axis=1) * acc_sc[:, gs] + pv
        return carry

    lax.fori_loop(0, n_groups, attn_body, 0)

    out_t = acc_sc[...] / l_sc[...]
    for pr in range(N_HEADS // 2):
        pair = jnp.concatenate([out_t[:, (2 * pr) * Q_TILE:(2 * pr + 1) * Q_TILE],
                                out_t[:, (2 * pr + 1) * Q_TILE:(2 * pr + 2) * Q_TILE]], axis=0)
        o_ref[0, :, pr * LANES:(pr + 1) * LANES] = pair.T.astype(BF16)


def _dsa(qT, qiT, wiT, k, ki, vT, *, topk):
    B, _, S = qT.shape
    nk = S // KEY_CHUNK
    return pl.pallas_call(
        functools.partial(_dsa_kernel, topk=topk),
        grid=(B, S // Q_TILE),
        in_specs=[
            pl.BlockSpec((1, ATT_Q, Q_TILE), lambda b, i: (b, 0, i)),
            pl.BlockSpec((1, IDX_Q, Q_TILE), lambda b, i: (b, 0, i)),
            pl.BlockSpec((1, SUBLANES, Q_TILE), lambda b, i: (b, 0, i)),
            pl.BlockSpec((1, S, ATT_KV), lambda b, i: (b, 0, 0)),
            pl.BlockSpec((1, S, LANES), lambda b, i: (b, 0, 0)),
            pl.BlockSpec((1, nk, ATT_KV, KEY_CHUNK), lambda b, i: (b, 0, 0, 0)),
        ],
        out_specs=pl.BlockSpec((1, Q_TILE, ATT_Q), lambda b, i: (b, i, 0)),
        out_shape=jax.ShapeDtypeStruct((B, S, ATT_Q), BF16),
        scratch_shapes=[
            pltpu.VMEM((nk, KEY_CHUNK, Q_TILE), F32),
            pltpu.VMEM((ATT_KV, N_HEADS * Q_TILE), BF16),
            pltpu.VMEM((LANES, IDX_HEADS * Q_TILE), BF16),
            pltpu.VMEM((HEAD_DIM, N_HEADS * Q_TILE), F32),
            pltpu.VMEM((1, N_HEADS * Q_TILE), F32),
            pltpu.VMEM((1, N_HEADS * Q_TILE), F32),
        ],
        compiler_params=pltpu.CompilerParams(
            dimension_semantics=("parallel", "arbitrary"), vmem_limit_bytes=VMEM_LIMIT),
        name="dsa",
    )(qT, qiT, wiT, k, ki, vT)


def _post_kernel(attn_ref, sa_ref, pb_ref, x_ref, woa_ref, wout_ref, g1_ref, b1_ref,
                 w1_ref, w2_ref, g2_ref, b2_ref, o_ref, *, alpha):
    ya = _dot(attn_ref[0], woa_ref[...])
    merged = sa_ref[0].astype(F32) * ya + pb_ref[0].astype(F32)
    mix = _dot(merged.astype(BF16), wout_ref[...])
    x1 = _layer_norm(alpha * x_ref[0] + mix, g1_ref[...], b1_ref[...])
    xb = x1.astype(BF16)
    acc = jnp.zeros(x1.shape, F32)
    for cc in range(D_FF // FF_CHUNK):
        h = _dot(xb, w1_ref[:, cc * FF_CHUNK:(cc + 1) * FF_CHUNK])
        h = jnp.square(jnp.maximum(h, 0.0)).astype(BF16)
        acc = acc + _dot(h, w2_ref[cc * FF_CHUNK:(cc + 1) * FF_CHUNK, :])
    o_ref[0] = _layer_norm(alpha * x1 + acc, g2_ref[...], b2_ref[...])


def _post(attn, sa, pb, x, woa, wout, g1, b1, w1, w2, g2, b2, *, alpha):
    B, S, D = x.shape
    tm = POST_TOKENS
    tok = lambda width: pl.BlockSpec((1, tm, width), lambda b, t: (b, t, 0))
    return pl.pallas_call(
        functools.partial(_post_kernel, alpha=alpha),
        grid=(B, S // tm),
        in_specs=[tok(ATT_Q), tok(D), tok(D), tok(D),
                  _resident((ATT_Q, D)), _resident((D, D)), _resident((1, D)), _resident((1, D)),
                  _resident((D, D_FF)), _resident((D_FF, D)), _resident((1, D)), _resident((1, D))],
        out_specs=tok(D),
        out_shape=jax.ShapeDtypeStruct((B, S, D), F32),
        compiler_params=pltpu.CompilerParams(
            dimension_semantics=("parallel", "parallel"), vmem_limit_bytes=VMEM_LIMIT),
        name="post",
    )(attn, sa, pb, x, woa, wout, g1, b1, w1, w2, g2, b2)


def kernel(x, positions, w_in, w_s, b_s, ln_v_g, ln_v_b, w_oa, w_ob, w_out,
           ln1_g, ln1_b, w_ff1, w_ff2, ln2_g, ln2_b):
    B, S, D = x.shape
    depth = w_in.shape[0]
    assert D == D_MODEL and S % max(IN_PROJ_TOKENS, POST_TOKENS, GROUP_KEYS) == 0
    assert w_in.shape[2] == COL_T_END + COLS_S
    alpha = float((2 * depth) ** 0.25)
    topk = min(TOPK_MAX, S // 4)
    cosT, sinT = _rope_tables(positions)
    wt_all = _transposed_in_weights(w_in)
    row2 = lambda v: v.reshape(1, -1)
    for l in range(depth):
        ws = w_in[l, :, COL_T_END:].astype(BF16)
        bs_b = jnp.repeat(b_s[l].T, SGU_WIDTH // SGU_GROUPS, axis=1)
        qT, k, vT, qiT, ki, wiT, sa, pb = _in_proj(
            x, cosT, sinT, wt_all[l], ws, w_ob[l].astype(BF16), w_s[l], bs_b,
            row2(ln_v_g[l]), row2(ln_v_b[l]))
        attn = _dsa(qT, qiT, wiT, k, ki, vT, topk=topk)
        x = _post(attn, sa, pb, x, w_oa[l].astype(BF16), w_out[l].astype(BF16),
                  row2(ln1_g[l]), row2(ln1_b[l]), w_ff1[l].astype(BF16), w_ff2[l].astype(BF16),
                  row2(ln2_g[l]), row2(ln2_b[l]), alpha=alpha)
    return x
```

```python
import functools

import numpy as np
import jax
import jax.numpy as jnp
from jax import lax
from jax.experimental import pallas as pl
from jax.experimental.pallas import tpu as pltpu

F32 = jnp.float32
BF16 = jnp.bfloat16
I32 = jnp.int32

D_MODEL = 1024
N_HEADS = 8
N_KV_HEADS = 2
HEAD_DIM = 64
ROPE_DIM = HEAD_DIM // 4
ROPE_HALF = ROPE_DIM // 2
ROPE_THETA = 500000.0
IDX_HEADS = 4
IDX_DIM = 64
TOPK_MAX = 256
SGU_CHUNK = 128
SGU_GROUPS = 4
SGU_WIDTH = 512
D_FF = 4 * D_MODEL
LN_EPS = 1e-5

ATT_Q = N_HEADS * HEAD_DIM
ATT_KV = N_KV_HEADS * HEAD_DIM
IDX_Q = IDX_HEADS * IDX_DIM
HEADS_PER_KV = N_HEADS // N_KV_HEADS

LANES = 128
SUBLANES = 8
KEY_CHUNK = LANES
Q_TILE = 2 * LANES
GROUP = 4
GROUP_KEYS = GROUP * KEY_CHUNK
IN_PROJ_TOKENS = 512
POST_TOKENS = 512
FF_CHUNK = 512
VMEM_LIMIT = 56 * 1024 * 1024

LOG2E = float(np.log2(np.e))
ATT_SCALE = HEAD_DIM ** -0.5
IDX_SCALE = (IDX_HEADS ** -0.5) * (IDX_DIM ** -0.5)
NEG_BIG = -1e30

KEY_NEG_INF = np.int32(np.array(0xFF800000, np.uint32).view(np.int32) ^ np.int32(0x7FFFFFFF))
KEY_POS_INF = np.int32(0x7F800000)
INT_MIN = np.int32(-2 ** 31)
KEY16_POS_INF = np.int32(0x7F80)
KEY16_NEG_INF = np.int32(-0x7F80 - 1)
BF16_ROWS = 16
FINE_ROUNDS = 17

ROW_Q = 0
ROW_K = ROW_Q + ATT_Q
ROW_V = ROW_K + ATT_KV
ROW_QI = ROW_V + ATT_KV
ROW_KI = ROW_QI + IDX_Q
ROWS_T = ROW_KI + LANES
COL_T_END = ROW_KI + IDX_DIM + IDX_HEADS
COL_U = 0
COL_VG = COL_U + SGU_WIDTH
COL_GA = COL_VG + SGU_WIDTH
COL_GB = COL_GA + D_MODEL
COLS_S = COL_GB + D_MODEL


def _layer_norm(z, g, b):
    mu = jnp.mean(z, axis=-1, keepdims=True)
    zc = z - mu
    var = jnp.mean(zc * zc, axis=-1, keepdims=True)
    return zc * lax.rsqrt(var + LN_EPS) * g + b


def _gelu(z):
    return 0.5 * z * (1.0 + lax.erf(z * np.float32(np.sqrt(0.5))))


def _dot(a, b):
    return jnp.dot(a, b, preferred_element_type=F32)


def _dot_nt(a, b):
    return lax.dot_general(a, b, (((1,), (1,)), ((), ())), preferred_element_type=F32)


def _resident(shape):
    return pl.BlockSpec(shape, lambda *_: (0,) * len(shape), pipeline_mode=pl.Buffered(1))


def _rope_kernel(pos_ref, inv_ref, cos_ref, sin_ref):
    ang = pos_ref[0].astype(F32) * inv_ref[:, 0:1]
    cos_ref[0] = jnp.cos(ang)
    sin_ref[0] = jnp.sin(ang)


def _rope_tables(positions):
    B, S = positions.shape
    inv_freq = ROPE_THETA ** (-jnp.arange(0, ROPE_DIM, 2, dtype=F32) / ROPE_DIM)
    inv_b = jnp.broadcast_to(inv_freq[:, None], (ROPE_HALF, LANES))
    return pl.pallas_call(
        _rope_kernel,
        grid=(B,),
        in_specs=[pl.BlockSpec((1, 1, S), lambda b: (b, 0, 0)),
                  pl.BlockSpec((ROPE_HALF, LANES), lambda b: (0, 0))],
        out_specs=[pl.BlockSpec((1, ROPE_HALF, S), lambda b: (b, 0, 0)),
                   pl.BlockSpec((1, ROPE_HALF, S), lambda b: (b, 0, 0))],
        out_shape=[jax.ShapeDtypeStruct((B, ROPE_HALF, S), F32)] * 2,
        name="rope_tables",
    )(positions.reshape(B, 1, S), inv_b)


def _wt_kernel(w_ref, o_ref):
    o_ref[0] = w_ref[0].T.astype(BF16)


def _transposed_in_weights(w_in):
    depth, d, _ = w_in.shape
    return pl.pallas_call(
        _wt_kernel,
        grid=(depth, ROWS_T // LANES),
        in_specs=[pl.BlockSpec((1, d, LANES), lambda l, c: (l, 0, c))],
        out_specs=pl.BlockSpec((1, LANES, d), lambda l, c: (l, c, 0)),
        out_shape=jax.ShapeDtypeStruct((depth, ROWS_T, d), BF16),
        name="w_in_transpose",
    )(w_in)


def _rope_rows(t, n_heads, c, s):
    pieces = []
    for h in range(n_heads):
        base = h * HEAD_DIM
        x1 = t[base:base + ROPE_HALF]
        x2 = t[base + ROPE_HALF:base + ROPE_DIM]
        pieces += [x1 * c - x2 * s, x2 * c + x1 * s, t[base + ROPE_DIM:base + HEAD_DIM]]
    return jnp.concatenate(pieces, axis=0)


def _in_proj_kernel(x_ref, cos_ref, sin_ref, wt_ref, ws_ref, wob_ref, wsgu_ref, bs_ref, lng_ref, lnb_ref,
                    qT_ref, k_ref, vT_ref, qiT_ref, ki_ref, wiT_ref, sa_ref, pb_ref, *, tm):
    xb = x_ref[0].astype(BF16)
    c = cos_ref[0]
    s = sin_ref[0]

    def proj_t(lo, hi):
        return _dot_nt(wt_ref[lo:hi, :], xb)

    qT_ref[0] = (_rope_rows(proj_t(ROW_Q, ROW_K), N_HEADS, c, s) * (ATT_SCALE * LOG2E)).astype(BF16)
    k_ref[0] = _rope_rows(proj_t(ROW_K, ROW_V), N_KV_HEADS, c, s).T.astype(BF16)
    vT = proj_t(ROW_V, ROW_QI).astype(BF16)
    for cc in range(tm // KEY_CHUNK):
        vT_ref[0, cc] = vT[:, cc * KEY_CHUNK:(cc + 1) * KEY_CHUNK]
    qiT_ref[0] = _rope_rows(proj_t(ROW_QI, ROW_KI), IDX_HEADS, c, s).astype(BF16)
    kiT = proj_t(ROW_KI, ROWS_T)
    wiT_ref[0] = kiT[IDX_DIM:IDX_DIM + SUBLANES] * IDX_SCALE
    kiT = jnp.concatenate([_rope_rows(kiT[0:IDX_DIM], 1, c, s), kiT[IDX_DIM:]], axis=0)
    ki_ref[0] = kiT.T.astype(BF16)

    u = _gelu(_dot(xb, ws_ref[:, COL_U:COL_VG]))
    vg = _gelu(_dot(xb, ws_ref[:, COL_VG:COL_GA]))
    vn = _layer_norm(vg, lng_ref[...], lnb_ref[...]).astype(BF16)
    row = lax.broadcasted_iota(I32, (SGU_CHUNK, SGU_CHUNK), 0)
    col = lax.broadcasted_iota(I32, (SGU_CHUNK, SGU_CHUNK), 1)
    w_tril = [jnp.where(row >= col, wsgu_ref[g], 0.0).astype(BF16) for g in range(SGU_GROUPS)]
    gd = SGU_WIDTH // SGU_GROUPS
    chunks = []
    for cc in range(tm // SGU_CHUNK):
        r0 = cc * SGU_CHUNK
        cols = [_dot(w_tril[g], vn[r0:r0 + SGU_CHUNK, g * gd:(g + 1) * gd]) for g in range(SGU_GROUPS)]
        chunks.append(jnp.concatenate(cols, axis=1) + bs_ref[...])
    sgu = u * jnp.concatenate(chunks, axis=0)
    yb = _dot(sgu.astype(BF16), wob_ref[...])

    sa_ref[0] = jax.nn.sigmoid(_dot(xb, ws_ref[:, COL_GA:COL_GB])).astype(BF16)
    pb_ref[0] = (jax.nn.sigmoid(_dot(xb, ws_ref[:, COL_GB:COLS_S])) * yb).astype(BF16)


def _in_proj(x, cosT, sinT, wt, ws, wob, wsgu, bs_b, lng, lnb):
    B, S, D = x.shape
    tm = IN_PROJ_TOKENS
    out_shape = [
        jax.ShapeDtypeStruct((B, ATT_Q, S), BF16),
        jax.ShapeDtypeStruct((B, S, ATT_KV), BF16),
        jax.ShapeDtypeStruct((B, S // KEY_CHUNK, ATT_KV, KEY_CHUNK), BF16),
        jax.ShapeDtypeStruct((B, IDX_Q, S), BF16),
        jax.ShapeDtypeStruct((B, S, LANES), BF16),
        jax.ShapeDtypeStruct((B, SUBLANES, S), F32),
        jax.ShapeDtypeStruct((B, S, D), BF16),
        jax.ShapeDtypeStruct((B, S, D), BF16),
    ]
    out_specs = [
        pl.BlockSpec((1, ATT_Q, tm), lambda b, t: (b, 0, t)),
        pl.BlockSpec((1, tm, ATT_KV), lambda b, t: (b, t, 0)),
        pl.BlockSpec((1, tm // KEY_CHUNK, ATT_KV, KEY_CHUNK), lambda b, t: (b, t, 0, 0)),
        pl.BlockSpec((1, IDX_Q, tm), lambda b, t: (b, 0, t)),
        pl.BlockSpec((1, tm, LANES), lambda b, t: (b, t, 0)),
        pl.BlockSpec((1, SUBLANES, tm), lambda b, t: (b, 0, t)),
        pl.BlockSpec((1, tm, D), lambda b, t: (b, t, 0)),
        pl.BlockSpec((1, tm, D), lambda b, t: (b, t, 0)),
    ]
    in_specs = [
        pl.BlockSpec((1, tm, D), lambda b, t: (b, t, 0)),
        pl.BlockSpec((1, ROPE_HALF, tm), lambda b, t: (b, 0, t)),
        pl.BlockSpec((1, ROPE_HALF, tm), lambda b, t: (b, 0, t)),
        _resident((ROWS_T, D)),
        _resident((D, COLS_S)),
        _resident((SGU_WIDTH, D)),
        _resident((SGU_GROUPS, SGU_CHUNK, SGU_CHUNK)),
        _resident((SGU_CHUNK, SGU_WIDTH)),
        _resident((1, SGU_WIDTH)),
        _resident((1, SGU_WIDTH)),
    ]
    return pl.pallas_call(
        functools.partial(_in_proj_kernel, tm=tm),
        grid=(B, S // tm),
        in_specs=in_specs,
        out_specs=out_specs,
        out_shape=out_shape,
        compiler_params=pltpu.CompilerParams(
            dimension_semantics=("parallel", "parallel"), vmem_limit_bytes=VMEM_LIMIT),
        name="in_proj",
    )(x, cosT, sinT, wt, ws, wob, wsgu, bs_b, lng, lnb)


def _sum_rows_to_vreg(v):
    parts = [v[r:r + SUBLANES] for r in range(0, v.shape[0], SUBLANES)]
    while len(parts) > 1:
        nxt = [parts[a] + parts[a + 1] for a in range(0, len(parts) - 1, 2)]
        if len(parts) % 2:
            nxt.append(parts[-1])
        parts = nxt
    return parts[0]


def _key_to_float(key):
    key = jnp.clip(key, KEY_NEG_INF, KEY_POS_INF)
    return pltpu.bitcast(key ^ ((key >> 31) & np.int32(0x7FFFFFFF)), F32)


def _float_to_key(x):
    bits = pltpu.bitcast(x, I32)
    return bits ^ ((bits >> 31) & np.int32(0x7FFFFFFF))


def _key16_to_float(key16):
    key16 = jnp.clip(key16, KEY16_NEG_INF, KEY16_POS_INF)
    bits16 = key16 ^ ((key16 >> 15) & np.int32(0x7FFF))
    return pltpu.bitcast(lax.shift_left(bits16, np.int32(16)), F32)


def _dsa_kernel(qT_ref, qiT_ref, wiT_ref, k_ref, ki_ref, vT_ref, o_ref,
                sc_sc, sc16_sc, rq_sc, ri_sc, acc_sc, m_sc, l_sc, *, topk):
    i = pl.program_id(1)
    n_groups = ((i + 1) * Q_TILE + GROUP_KEYS - 1) // GROUP_KEYS

    rq_sc[...] = jnp.zeros(rq_sc.shape, BF16)
    for h in range(N_HEADS):
        g = h // HEADS_PER_KV
        rq_sc[g * HEAD_DIM:(g + 1) * HEAD_DIM, h * Q_TILE:(h + 1) * Q_TILE] = \
            qT_ref[0, h * HEAD_DIM:(h + 1) * HEAD_DIM, :]
    ri_sc[...] = jnp.zeros(ri_sc.shape, BF16)
    for h in range(IDX_HEADS):
        ri_sc[0:IDX_DIM, h * Q_TILE:(h + 1) * Q_TILE] = qiT_ref[0, h * IDX_DIM:(h + 1) * IDX_DIM, :]

    w = wiT_ref[0]
    row_c = lax.broadcasted_iota(I32, (KEY_CHUNK, Q_TILE), 0)
    col_c = lax.broadcasted_iota(I32, (KEY_CHUNK, Q_TILE), 1)

    def group_rows(jj):
        return pl.ds(pl.multiple_of(jj * GROUP_KEYS, GROUP_KEYS), GROUP_KEYS)

    def load_group(jj):
        return jnp.concatenate([sc_sc[jj * GROUP + cc] for cc in range(GROUP)], axis=0)

    def score_body(jj, carry):
        for cc in range(GROUP):
            ch = jj * GROUP + cc
            kc = ki_ref[0, pl.ds(pl.multiple_of(ch * KEY_CHUNK, KEY_CHUNK), KEY_CHUNK), :]
            lg = _dot(kc, ri_sc[...])
            sc = w[0:1, :] * jnp.maximum(lg[:, 0:Q_TILE], 0.0)
            for h in range(1, IDX_HEADS):
                sc = sc + w[h:h + 1, :] * jnp.maximum(lg[:, h * Q_TILE:(h + 1) * Q_TILE], 0.0)
            sc = jnp.where(ch * KEY_CHUNK + row_c <= i * Q_TILE + col_c, sc, -jnp.inf)
            sc_sc[ch] = sc
            sc16_sc[ch] = sc.astype(BF16)
        return carry

    lax.fori_loop(0, n_groups, score_body, 0)

    def count_pass(pred):
        def body(jj, acc):
            for cc in range(GROUP):
                blk = sc_sc[jj * GROUP + cc]
                acc = acc + _sum_rows_to_vreg(jnp.where(pred(blk), 1, 0).astype(I32))
            return acc
        acc = lax.fori_loop(0, n_groups, body, jnp.zeros((SUBLANES, Q_TILE), I32))
        return jnp.sum(acc, axis=0, keepdims=True)

    def count16_pass(cand_f):
        cand = jnp.broadcast_to(cand_f, (BF16_ROWS, Q_TILE)).astype(BF16)
        one = jnp.ones((BF16_ROWS, Q_TILE), BF16)
        zero = jnp.zeros((BF16_ROWS, Q_TILE), BF16)

        def body(jj, acc):
            for cc in range(GROUP):
                blk = sc16_sc[jj * GROUP + cc]
                parts = [jnp.where(blk[r * BF16_ROWS:(r + 1) * BF16_ROWS] >= cand, one, zero)
                         for r in range(KEY_CHUNK // BF16_ROWS)]
                while len(parts) > 1:
                    parts = [parts[a] + parts[a + 1] for a in range(0, len(parts), 2)]
                acc = acc + parts[0]
            return acc
        acc = lax.fori_loop(0, n_groups, body, zero)
        return jnp.sum(acc.astype(F32), axis=0, keepdims=True)

    def coarse_body(r, t16):
        cand = t16 + lax.shift_left(np.int32(1), 15 - r)
        total = count16_pass(_key16_to_float(cand))
        return jnp.where(total >= topk, cand, t16)

    t16 = lax.fori_loop(0, 16, coarse_body, jnp.full((1, Q_TILE), -2 ** 15, I32))
    k32 = _float_to_key(_key16_to_float(t16))
    lo0 = jnp.maximum(k32, KEY_NEG_INF + 2 ** 15 + 1) - (2 ** 15 + 1)
    hi0 = jnp.minimum(k32, KEY_POS_INF - 2 ** 16 - 1) + (2 ** 16 + 1)

    def fine_body(r, lohi):
        lo, hi = lohi
        mid = lo + lax.shift_right_arithmetic(hi - lo, np.int32(1))
        mid_f = _key_to_float(mid)
        ge = count_pass(lambda blk: blk >= mid_f) >= topk
        return jnp.where(ge, mid, lo), jnp.where(ge, hi, mid)

    t_key, _ = lax.fori_loop(0, FINE_ROUNDS, fine_body, (lo0, hi0))
    t = _key_to_float(t_key)
    n_gt = count_pass(lambda blk: blk > t)
    n_ties = (topk - n_gt).astype(F32)

    tri = jnp.where(lax.broadcasted_iota(I32, (KEY_CHUNK, KEY_CHUNK), 0)
                    >= lax.broadcasted_iota(I32, (KEY_CHUNK, KEY_CHUNK), 1), 1.0, 0.0).astype(BF16)

    def mask_body(jj, tie_carry):
        for cc in range(GROUP):
            ch = jj * GROUP + cc
            sc = sc_sc[ch]
            eq = sc == t
            pref = _dot(tri, jnp.where(eq, 1.0, 0.0).astype(BF16)) + tie_carry
            tie_carry = pref[KEY_CHUNK - 1:KEY_CHUNK, :]
            sel = ((sc > t) | (eq & (pref <= n_ties))) & (ch * KEY_CHUNK + row_c <= i * Q_TILE + col_c)
            sc_sc[ch] = jnp.where(sel, 0.0, NEG_BIG)
        return tie_carry

    lax.fori_loop(0, n_groups, mask_body, jnp.zeros((1, Q_TILE), F32))

    m_sc[...] = jnp.full(m_sc.shape, NEG_BIG, F32)
    l_sc[...] = jnp.zeros(l_sc.shape, F32)
    acc_sc[...] = jnp.zeros(acc_sc.shape, F32)
    kv_cols = HEADS_PER_KV * Q_TILE

    def attn_body(jj, carry):
        bias = load_group(jj)
        s = _dot(k_ref[0, group_rows(jj), :], rq_sc[...])
        for g in range(N_KV_HEADS):
            ps, alphas = [], []
            for h in range(g * HEADS_PER_KV, (g + 1) * HEADS_PER_KV):
                cs = slice(h * Q_TILE, (h + 1) * Q_TILE)
                sm = s[:, cs] + bias
                m_old = m_sc[:, cs]
                m_new = jnp.maximum(m_old, jnp.max(sm, axis=0, keepdims=True))
                alpha = jnp.exp2(m_old - m_new)
                p = jnp.exp2(sm - m_new)
                l_sc[:, cs] = alpha * l_sc[:, cs] + jnp.sum(p, axis=0, keepdims=True)
                m_sc[:, cs] = m_new
                ps.append(p.astype(BF16))
                alphas.append(alpha)
            gs = slice(g * kv_cols, (g + 1) * kv_cols)
            vt = jnp.concatenate([vT_ref[0, jj * GROUP + cc, g * HEAD_DIM:(g + 1) * HEAD_DIM, :]
                                  for cc in range(GROUP)], axis=1)
            pv = _dot(vt, jnp.concatenate(ps, axis=1))
            acc_sc[:, gs] = jnp.concatenate(alphas, axis=1) * acc_sc[:, gs] + pv
        return carry

    lax.fori_loop(0, n_groups, attn_body, 0)

    out_t = acc_sc[...] / l_sc[...]
    for pr in range(N_HEADS // 2):
        pair = jnp.concatenate([out_t[:, (2 * pr) * Q_TILE:(2 * pr + 1) * Q_TILE],
                                out_t[:, (2 * pr + 1) * Q_TILE:(2 * pr + 2) * Q_TILE]], axis=0)
        o_ref[0, :, pr * LANES:(pr + 1) * LANES] = pair.T.astype(BF16)


def _dsa(qT, qiT, wiT, k, ki, vT, *, topk):
    B, _, S = qT.shape
    nk = S // KEY_CHUNK
    return pl.pallas_call(
        functools.partial(_dsa_kernel, topk=topk),
        grid=(B, S // Q_TILE),
        in_specs=[
            pl.BlockSpec((1, ATT_Q, Q_TILE), lambda b, i: (b, 0, i)),
            pl.BlockSpec((1, IDX_Q, Q_TILE), lambda b, i: (b, 0, i)),
            pl.BlockSpec((1, SUBLANES, Q_TILE), lambda b, i: (b, 0, i)),
            pl.BlockSpec((1, S, ATT_KV), lambda b, i: (b, 0, 0)),
            pl.BlockSpec((1, S, LANES), lambda b, i: (b, 0, 0)),
            pl.BlockSpec((1, nk, ATT_KV, KEY_CHUNK), lambda b, i: (b, 0, 0, 0)),
        ],
        out_specs=pl.BlockSpec((1, Q_TILE, ATT_Q), lambda b, i: (b, i, 0)),
        out_shape=jax.ShapeDtypeStruct((B, S, ATT_Q), BF16),
        scratch_shapes=[
            pltpu.VMEM((nk, KEY_CHUNK, Q_TILE), F32),
            pltpu.VMEM((nk, KEY_CHUNK, Q_TILE), BF16),
            pltpu.VMEM((ATT_KV, N_HEADS * Q_TILE), BF16),
            pltpu.VMEM((LANES, IDX_HEADS * Q_TILE), BF16),
            pltpu.VMEM((HEAD_DIM, N_HEADS * Q_TILE), F32),
            pltpu.VMEM((1, N_HEADS * Q_TILE), F32),
            pltpu.VMEM((1, N_HEADS * Q_TILE), F32),
        ],
        compiler_params=pltpu.CompilerParams(
            dimension_semantics=("parallel", "arbitrary"), vmem_limit_bytes=VMEM_LIMIT),
        name="dsa",
    )(qT, qiT, wiT, k, ki, vT)


def _post_kernel(attn_ref, sa_ref, pb_ref, x_ref, woa_ref, wout_ref, g1_ref, b1_ref,
                 w1_ref, w2_ref, g2_ref, b2_ref, o_ref, *, alpha):
    ya = _dot(attn_ref[0], woa_ref[...])
    merged = sa_ref[0].astype(F32) * ya + pb_ref[0].astype(F32)
    mix = _dot(merged.astype(BF16), wout_ref[...])
    x1 = _layer_norm(alpha * x_ref[0] + mix, g1_ref[...], b1_ref[...])
    xb = x1.astype(BF16)
    acc = jnp.zeros(x1.shape, F32)
    for cc in range(D_FF // FF_CHUNK):
        h = _dot(xb, w1_ref[:, cc * FF_CHUNK:(cc + 1) * FF_CHUNK])
        h = jnp.square(jnp.maximum(h, 0.0)).astype(BF16)
        acc = acc + _dot(h, w2_ref[cc * FF_CHUNK:(cc + 1) * FF_CHUNK, :])
    o_ref[0] = _layer_norm(alpha * x1 + acc, g2_ref[...], b2_ref[...])


def _post(attn, sa, pb, x, woa, wout, g1, b1, w1, w2, g2, b2, *, alpha):
    B, S, D = x.shape
    tm = POST_TOKENS
    tok = lambda width: pl.BlockSpec((1, tm, width), lambda b, t: (b, t, 0))
    return pl.pallas_call(
        functools.partial(_post_kernel, alpha=alpha),
        grid=(B, S // tm),
        in_specs=[tok(ATT_Q), tok(D), tok(D), tok(D),
                  _resident((ATT_Q, D)), _resident((D, D)), _resident((1, D)), _resident((1, D)),
                  _resident((D, D_FF)), _resident((D_FF, D)), _resident((1, D)), _resident((1, D))],
        out_specs=tok(D),
        out_shape=jax.ShapeDtypeStruct((B, S, D), F32),
        compiler_params=pltpu.CompilerParams(
            dimension_semantics=("parallel", "parallel"), vmem_limit_bytes=VMEM_LIMIT),
        name="post",
    )(attn, sa, pb, x, woa, wout, g1, b1, w1, w2, g2, b2)


def kernel(x, positions, w_in, w_s, b_s, ln_v_g, ln_v_b, w_oa, w_ob, w_out,
           ln1_g, ln1_b, w_ff1, w_ff2, ln2_g, ln2_b):
    B, S, D = x.shape
    depth = w_in.shape[0]
    assert D == D_MODEL and S % max(IN_PROJ_TOKENS, POST_TOKENS, GROUP_KEYS) == 0
    assert w_in.shape[2] == COL_T_END + COLS_S
    alpha = float((2 * depth) ** 0.25)
    topk = min(TOPK_MAX, S // 4)
    cosT, sinT = _rope_tables(positions)
    wt_all = _transposed_in_weights(w_in)
    row2 = lambda v: v.reshape(1, -1)
    for l in range(depth):
        ws = w_in[l, :, COL_T_END:].astype(BF16)
        bs_b = jnp.repeat(b_s[l].T, SGU_WIDTH // SGU_GROUPS, axis=1)
        qT, k, vT, qiT, ki, wiT, sa, pb = _in_proj(
            x, cosT, sinT, wt_all[l], ws, w_ob[l].astype(BF16), w_s[l], bs_b,
            row2(ln_v_g[l]), row2(ln_v_b[l]))
        attn = _dsa(qT, qiT, wiT, k, ki, vT, topk=topk)
        x = _post(attn, sa, pb, x, w_oa[l].astype(BF16), w_out[l].astype(BF16),
                  row2(ln1_g[l]), row2(ln1_b[l]), w_ff1[l].astype(BF16), w_ff2[l].astype(BF16),
                  row2(ln2_g[l]), row2(ln2_b[l]), alpha=alpha)
    return x
```

```python
import functools

import numpy as np
import jax
import jax.numpy as jnp
from jax import lax
from jax.experimental import pallas as pl
from jax.experimental.pallas import tpu as pltpu

F32 = jnp.float32
BF16 = jnp.bfloat16
I32 = jnp.int32

D_MODEL = 1024
N_HEADS = 8
N_KV_HEADS = 2
HEAD_DIM = 64
ROPE_DIM = HEAD_DIM // 4
ROPE_HALF = ROPE_DIM // 2
ROPE_THETA = 500000.0
IDX_HEADS = 4
IDX_DIM = 64
TOPK_MAX = 256
SGU_CHUNK = 128
SGU_GROUPS = 4
SGU_WIDTH = 512
D_FF = 4 * D_MODEL
LN_EPS = 1e-5

ATT_Q = N_HEADS * HEAD_DIM
ATT_KV = N_KV_HEADS * HEAD_DIM
IDX_Q = IDX_HEADS * IDX_DIM
HEADS_PER_KV = N_HEADS // N_KV_HEADS

LANES = 128
SUBLANES = 8
KEY_CHUNK = LANES
Q_TILE = 2 * LANES
GROUP = 4
GROUP_KEYS = GROUP * KEY_CHUNK
IN_PROJ_TOKENS = 512
POST_TOKENS = 512
FF_CHUNK = 512
VMEM_LIMIT = 56 * 1024 * 1024

LOG2E = float(np.log2(np.e))
ATT_SCALE = HEAD_DIM ** -0.5
IDX_SCALE = (IDX_HEADS ** -0.5) * (IDX_DIM ** -0.5)
NEG_BIG = -1e30

KEY_NEG_INF = np.int32(np.array(0xFF800000, np.uint32).view(np.int32) ^ np.int32(0x7FFFFFFF))
KEY_POS_INF = np.int32(0x7F800000)
INT_MIN = np.int32(-2 ** 31)
KEY16_POS_INF = np.int32(0x7F80)
KEY16_NEG_INF = np.int32(-0x7F80 - 1)
BF16_ROWS = 16
FINE_ROUNDS = 17

ROW_Q = 0
ROW_K = ROW_Q + ATT_Q
ROW_V = ROW_K + ATT_KV
ROW_QI = ROW_V + ATT_KV
ROW_KI = ROW_QI + IDX_Q
ROWS_T = ROW_KI + LANES
COL_T_END = ROW_KI + IDX_DIM + IDX_HEADS
COL_U = 0
COL_VG = COL_U + SGU_WIDTH
COL_GA = COL_VG + SGU_WIDTH
COL_GB = COL_GA + D_MODEL
COLS_S = COL_GB + D_MODEL


def _layer_norm(z, g, b):
    mu = jnp.mean(z, axis=-1, keepdims=True)
    zc = z - mu
    var = jnp.mean(zc * zc, axis=-1, keepdims=True)
    return zc * lax.rsqrt(var + LN_EPS) * g + b


def _gelu(z):
    return 0.5 * z * (1.0 + lax.erf(z * np.float32(np.sqrt(0.5))))


def _dot(a, b):
    return jnp.dot(a, b, preferred_element_type=F32)


def _dot_nt(a, b):
    return lax.dot_general(a, b, (((1,), (1,)), ((), ())), preferred_element_type=F32)


def _resident(shape):
    return pl.BlockSpec(shape, lambda *_: (0,) * len(shape), pipeline_mode=pl.Buffered(1))


def _rope_kernel(pos_ref, inv_ref, cos_ref, sin_ref):
    ang = pos_ref[0].astype(F32) * inv_ref[:, 0:1]
    cos_ref[0] = jnp.cos(ang)
    sin_ref[0] = jnp.sin(ang)


def _rope_tables(positions):
    B, S = positions.shape
    inv_freq = ROPE_THETA ** (-jnp.arange(0, ROPE_DIM, 2, dtype=F32) / ROPE_DIM)
    inv_b = jnp.broadcast_to(inv_freq[:, None], (ROPE_HALF, LANES))
    return pl.pallas_call(
        _rope_kernel,
        grid=(B,),
        in_specs=[pl.BlockSpec((1, 1, S), lambda b: (b, 0, 0)),
                  pl.BlockSpec((ROPE_HALF, LANES), lambda b: (0, 0))],
        out_specs=[pl.BlockSpec((1, ROPE_HALF, S), lambda b: (b, 0, 0)),
                   pl.BlockSpec((1, ROPE_HALF, S), lambda b: (b, 0, 0))],
        out_shape=[jax.ShapeDtypeStruct((B, ROPE_HALF, S), F32)] * 2,
        name="rope_tables",
    )(positions.reshape(B, 1, S), inv_b)


def _wt_kernel(w_ref, o_ref):
    o_ref[0] = w_ref[0].T.astype(BF16)


def _transposed_in_weights(w_in):
    depth, d, _ = w_in.shape
    return pl.pallas_call(
        _wt_kernel,
        grid=(depth, ROWS_T // LANES),
        in_specs=[pl.BlockSpec((1, d, LANES), lambda l, c: (l, 0, c))],
        out_specs=pl.BlockSpec((1, LANES, d), lambda l, c: (l, c, 0)),
        out_shape=jax.ShapeDtypeStruct((depth, ROWS_T, d), BF16),
        name="w_in_transpose",
    )(w_in)


def _rope_rows(t, n_heads, c, s):
    pieces = []
    for h in range(n_heads):
        base = h * HEAD_DIM
        x1 = t[base:base + ROPE_HALF]
        x2 = t[base + ROPE_HALF:base + ROPE_DIM]
        pieces += [x1 * c - x2 * s, x2 * c + x1 * s, t[base + ROPE_DIM:base + HEAD_DIM]]
    return jnp.concatenate(pieces, axis=0)


def _in_proj_kernel(x_ref, cos_ref, sin_ref, wt_ref, ws_ref, wob_ref, wsgu_ref, bs_ref, lng_ref, lnb_ref,
                    qT_ref, k_ref, vT_ref, qiT_ref, ki_ref, wiT_ref, sa_ref, pb_ref, *, tm):
    xb = x_ref[0].astype(BF16)
    c = cos_ref[0]
    s = sin_ref[0]

    def proj_t(lo, hi):
        return _dot_nt(wt_ref[lo:hi, :], xb)

    qT_ref[0] = (_rope_rows(proj_t(ROW_Q, ROW_K), N_HEADS, c, s) * (ATT_SCALE * LOG2E)).astype(BF16)
    k_ref[0] = _rope_rows(proj_t(ROW_K, ROW_V), N_KV_HEADS, c, s).T.astype(BF16)
    vT = proj_t(ROW_V, ROW_QI).astype(BF16)
    for cc in range(tm // KEY_CHUNK):
        vT_ref[0, cc] = vT[:, cc * KEY_CHUNK:(cc + 1) * KEY_CHUNK]
    qiT_ref[0] = _rope_rows(proj_t(ROW_QI, ROW_KI), IDX_HEADS, c, s).astype(BF16)
    kiT = proj_t(ROW_KI, ROWS_T)
    wiT_ref[0] = kiT[IDX_DIM:IDX_DIM + SUBLANES] * IDX_SCALE
    kiT = jnp.concatenate([_rope_rows(kiT[0:IDX_DIM], 1, c, s), kiT[IDX_DIM:]], axis=0)
    ki_ref[0] = kiT.T.astype(BF16)

    u = _gelu(_dot(xb, ws_ref[:, COL_U:COL_VG]))
    vg = _gelu(_dot(xb, ws_ref[:, COL_VG:COL_GA]))
    vn = _layer_norm(vg, lng_ref[...], lnb_ref[...]).astype(BF16)
    row = lax.broadcasted_iota(I32, (SGU_CHUNK, SGU_CHUNK), 0)
    col = lax.broadcasted_iota(I32, (SGU_CHUNK, SGU_CHUNK), 1)
    w_tril = [jnp.where(row >= col, wsgu_ref[g], 0.0).astype(BF16) for g in range(SGU_GROUPS)]
    gd = SGU_WIDTH // SGU_GROUPS
    chunks = []
    for cc in range(tm // SGU_CHUNK):
        r0 = cc * SGU_CHUNK
        cols = [_dot(w_tril[g], vn[r0:r0 + SGU_CHUNK, g * gd:(g + 1) * gd]) for g in range(SGU_GROUPS)]
        chunks.append(jnp.concatenate(cols, axis=1) + bs_ref[...])
    sgu = u * jnp.concatenate(chunks, axis=0)
    yb = _dot(sgu.astype(BF16), wob_ref[...])

    sa_ref[0] = jax.nn.sigmoid(_dot(xb, ws_ref[:, COL_GA:COL_GB])).astype(BF16)
    pb_ref[0] = (jax.nn.sigmoid(_dot(xb, ws_ref[:, COL_GB:COLS_S])) * yb).astype(BF16)


def _in_proj(x, cosT, sinT, wt, ws, wob, wsgu, bs_b, lng, lnb):
    B, S, D = x.shape
    tm = IN_PROJ_TOKENS
    out_shape = [
        jax.ShapeDtypeStruct((B, ATT_Q, S), BF16),
        jax.ShapeDtypeStruct((B, S, ATT_KV), BF16),
        jax.ShapeDtypeStruct((B, S // KEY_CHUNK, ATT_KV, KEY_CHUNK), BF16),
        jax.ShapeDtypeStruct((B, IDX_Q, S), BF16),
        jax.ShapeDtypeStruct((B, S, LANES), BF16),
        jax.ShapeDtypeStruct((B, SUBLANES, S), F32),
        jax.ShapeDtypeStruct((B, S, D), BF16),
        jax.ShapeDtypeStruct((B, S, D), BF16),
    ]
    out_specs = [
        pl.BlockSpec((1, ATT_Q, tm), lambda b, t: (b, 0, t)),
        pl.BlockSpec((1, tm, ATT_KV), lambda b, t: (b, t, 0)),
        pl.BlockSpec((1, tm // KEY_CHUNK, ATT_KV, KEY_CHUNK), lambda b, t: (b, t, 0, 0)),
        pl.BlockSpec((1, IDX_Q, tm), lambda b, t: (b, 0, t)),
        pl.BlockSpec((1, tm, LANES), lambda b, t: (b, t, 0)),
        pl.BlockSpec((1, SUBLANES, tm), lambda b, t: (b, 0, t)),
        pl.BlockSpec((1, tm, D), lambda b, t: (b, t, 0)),
        pl.BlockSpec((1, tm, D), lambda b, t: (b, t, 0)),
    ]
    in_specs = [
        pl.BlockSpec((1, tm, D), lambda b, t: (b, t, 0)),
        pl.BlockSpec((1, ROPE_HALF, tm), lambda b, t: (b, 0, t)),
        pl.BlockSpec((1, ROPE_HALF, tm), lambda b, t: (b, 0, t)),
        _resident((ROWS_T, D)),
        _resident((D, COLS_S)),
        _resident((SGU_WIDTH, D)),
        _resident((SGU_GROUPS, SGU_CHUNK, SGU_CHUNK)),
        _resident((SGU_CHUNK, SGU_WIDTH)),
        _resident((1, SGU_WIDTH)),
        _resident((1, SGU_WIDTH)),
    ]
    return pl.pallas_call(
        functools.partial(_in_proj_kernel, tm=tm),
        grid=(B, S // tm),
        in_specs=in_specs,
        out_specs=out_specs,
        out_shape=out_shape,
        compiler_params=pltpu.CompilerParams(
            dimension_semantics=("parallel", "parallel"), vmem_limit_bytes=VMEM_LIMIT),
        name="in_proj",
    )(x, cosT, sinT, wt, ws, wob, wsgu, bs_b, lng, lnb)


def _sum_rows_to_vreg(v):
    parts = [v[r:r + SUBLANES] for r in range(0, v.shape[0], SUBLANES)]
    while len(parts) > 1:
        nxt = [parts[a] + parts[a + 1] for a in range(0, len(parts) - 1, 2)]
        if len(parts) % 2:
            nxt.append(parts[-1])
        parts = nxt
    return parts[0]


def _key_to_float(key):
    key = jnp.clip(key, KEY_NEG_INF, KEY_POS_INF)
    return pltpu.bitcast(key ^ ((key >> 31) & np.int32(0x7FFFFFFF)), F32)


def _float_to_key(x):
    bits = pltpu.bitcast(x, I32)
    return bits ^ ((bits >> 31) & np.int32(0x7FFFFFFF))


def _key16_to_float(key16):
    key16 = jnp.clip(key16, KEY16_NEG_INF, KEY16_POS_INF)
    bits16 = key16 ^ ((key16 >> 15) & np.int32(0x7FFF))
    return pltpu.bitcast(lax.shift_left(bits16, np.int32(16)), F32)


def _dsa_kernel(qT_ref, qiT_ref, wiT_ref, k_ref, ki_ref, vT_ref, o_ref,
                sc_sc, sc16_sc, rq_sc, ri_sc, acc_sc, m_sc, l_sc, *, topk):
    i = pl.program_id(1)
    n_groups = ((i + 1) * Q_TILE + GROUP_KEYS - 1) // GROUP_KEYS

    rq_sc[...] = jnp.zeros(rq_sc.shape, BF16)
    for h in range(N_HEADS):
        g = h // HEADS_PER_KV
        rq_sc[g * HEAD_DIM:(g + 1) * HEAD_DIM, h * Q_TILE:(h + 1) * Q_TILE] = \
            qT_ref[0, h * HEAD_DIM:(h + 1) * HEAD_DIM, :]
    ri_sc[...] = jnp.zeros(ri_sc.shape, BF16)
    for h in range(IDX_HEADS):
        ri_sc[0:IDX_DIM, h * Q_TILE:(h + 1) * Q_TILE] = qiT_ref[0, h * IDX_DIM:(h + 1) * IDX_DIM, :]

    w = wiT_ref[0]
    row_c = lax.broadcasted_iota(I32, (KEY_CHUNK, Q_TILE), 0)
    col_c = lax.broadcasted_iota(I32, (KEY_CHUNK, Q_TILE), 1)

    def group_rows(jj):
        return pl.ds(pl.multiple_of(jj * GROUP_KEYS, GROUP_KEYS), GROUP_KEYS)

    def score_body(jj, carry):
        for cc in range(GROUP):
            ch = jj * GROUP + cc
            kc = ki_ref[0, pl.ds(pl.multiple_of(ch * KEY_CHUNK, KEY_CHUNK), KEY_CHUNK), :]
            lg = _dot(kc, ri_sc[...])
            sc = w[0:1, :] * jnp.maximum(lg[:, 0:Q_TILE], 0.0)
            for h in range(1, IDX_HEADS):
                sc = sc + w[h:h + 1, :] * jnp.maximum(lg[:, h * Q_TILE:(h + 1) * Q_TILE], 0.0)
            sc = jnp.where(ch * KEY_CHUNK + row_c <= i * Q_TILE + col_c, sc, -jnp.inf)
            sc_sc[ch] = sc
            sc16_sc[ch] = sc.astype(BF16)
        return carry

    lax.fori_loop(0, n_groups, score_body, 0)

    def count_pass(pred):
        def body(jj, acc):
            for cc in range(GROUP):
                blk = sc_sc[jj * GROUP + cc]
                acc = acc + _sum_rows_to_vreg(jnp.where(pred(blk), 1, 0).astype(I32))
            return acc
        acc = lax.fori_loop(0, n_groups, body, jnp.zeros((SUBLANES, Q_TILE), I32))
        return jnp.sum(acc, axis=0, keepdims=True)

    def count16_pass(cand_f):
        cand = jnp.broadcast_to(cand_f, (BF16_ROWS, Q_TILE)).astype(BF16)
        one = jnp.ones((BF16_ROWS, Q_TILE), BF16)
        zero = jnp.zeros((BF16_ROWS, Q_TILE), BF16)

        def body(jj, acc):
            for cc in range(GROUP):
                blk = sc16_sc[jj * GROUP + cc]
                parts = [jnp.where(blk[r * BF16_ROWS:(r + 1) * BF16_ROWS] >= cand, one, zero)
                         for r in range(KEY_CHUNK // BF16_ROWS)]
                while len(parts) > 1:
                    parts = [parts[a] + parts[a + 1] for a in range(0, len(parts), 2)]
                acc = acc + parts[0]
            return acc
        acc = lax.fori_loop(0, n_groups, body, zero)
        return jnp.sum(acc.astype(F32), axis=0, keepdims=True)

    def coarse_body(r, t16):
        cand = t16 + lax.shift_left(np.int32(1), 15 - r)
        total = count16_pass(_key16_to_float(cand))
        return jnp.where(total >= topk, cand, t16)

    t16 = lax.fori_loop(0, 16, coarse_body, jnp.full((1, Q_TILE), -2 ** 15, I32))
    k32 = _float_to_key(_key16_to_float(t16))
    lo0 = jnp.maximum(k32, KEY_NEG_INF + 2 ** 15 + 1) - (2 ** 15 + 1)
    hi0 = jnp.minimum(k32, KEY_POS_INF - 2 ** 16 - 1) + (2 ** 16 + 1)

    def fine_body(r, lohi):
        lo, hi = lohi
        mid = lo + lax.shift_right_arithmetic(hi - lo, np.int32(1))
        mid_f = _key_to_float(mid)
        ge = count_pass(lambda blk: blk >= mid_f) >= topk
        return jnp.where(ge, mid, lo), jnp.where(ge, hi, mid)

    t_key, _ = lax.fori_loop(0, FINE_ROUNDS, fine_body, (lo0, hi0))
    t = _key_to_float(t_key)
    n_gt = count_pass(lambda blk: blk > t)
    n_ties = (topk - n_gt).astype(F32)

    tri = jnp.where(lax.broadcasted_iota(I32, (KEY_CHUNK, KEY_CHUNK), 0)
                    >= lax.broadcasted_iota(I32, (KEY_CHUNK, KEY_CHUNK), 1), 1.0, 0.0).astype(BF16)

    def mask_body(jj, tie_carry):
        for cc in range(GROUP):
            ch = jj * GROUP + cc
            sc = sc_sc[ch]
            eq = sc == t
            pref = _dot(tri, jnp.where(eq, 1.0, 0.0).astype(BF16)) + tie_carry
            tie_carry = pref[KEY_CHUNK - 1:KEY_CHUNK, :]
            sel = ((sc > t) | (eq & (pref <= n_ties))) & (ch * KEY_CHUNK + row_c <= i * Q_TILE + col_c)
            sc_sc[ch] = jnp.where(sel, 0.0, NEG_BIG)
        return tie_carry

    lax.fori_loop(0, n_groups, mask_body, jnp.zeros((1, Q_TILE), F32))

    m_sc[...] = jnp.full(m_sc.shape, NEG_BIG, F32)
    l_sc[...] = jnp.zeros(l_sc.shape, F32)
    acc_sc[...] = jnp.zeros(acc_sc.shape, F32)
    kv_cols = HEADS_PER_KV * Q_TILE

    def attn_body(jj, carry):
        s = _dot(k_ref[0, group_rows(jj), :], rq_sc[...])
        for g in range(N_KV_HEADS):
            gs = slice(g * kv_cols, (g + 1) * kv_cols)
            m_old = m_sc[:, gs]
            mcs, lcs, pvs = [], [], []
            for cc in range(GROUP):
                bias = sc_sc[jj * GROUP + cc]
                ps, mh, lh = [], [], []
                for h in range(g * HEADS_PER_KV, (g + 1) * HEADS_PER_KV):
                    smc = s[cc * KEY_CHUNK:(cc + 1) * KEY_CHUNK, h * Q_TILE:(h + 1) * Q_TILE] + bias
                    mc = jnp.max(smc, axis=0, keepdims=True)
                    pc = jnp.exp2(smc - mc)
                    lh.append(jnp.sum(pc, axis=0, keepdims=True))
                    mh.append(mc)
                    ps.append(pc.astype(BF16))
                vt = vT_ref[0, jj * GROUP + cc, g * HEAD_DIM:(g + 1) * HEAD_DIM, :]
                pvs.append(_dot(vt, jnp.concatenate(ps, axis=1)))
                mcs.append(jnp.concatenate(mh, axis=1))
                lcs.append(jnp.concatenate(lh, axis=1))
            m_new = m_old
            for mc in mcs:
                m_new = jnp.maximum(m_new, mc)
            alpha = jnp.exp2(m_old - m_new)
            l_new = alpha * l_sc[:, gs]
            acc = alpha * acc_sc[:, gs]
            for cc in range(GROUP):
                f = jnp.exp2(mcs[cc] - m_new)
                l_new = l_new + f * lcs[cc]
                acc = acc + f * pvs[cc]
            l_sc[:, gs] = l_new
            m_sc[:, gs] = m_new
            acc_sc[:, gs] = acc
        return carry

    lax.fori_loop(0, n_groups, attn_body, 0)

    out_t = acc_sc[...] / l_sc[...]
    for pr in range(N_HEADS // 2):
        pair = jnp.concatenate([out_t[:, (2 * pr) * Q_TILE:(2 * pr + 1) * Q_TILE],
                                out_t[:, (2 * pr + 1) * Q_TILE:(2 * pr + 2) * Q_TILE]], axis=0)
        o_ref[0, :, pr * LANES:(pr + 1) * LANES] = pair.T.astype(BF16)


def _dsa(qT, qiT, wiT, k, ki, vT, *, topk):
    B, _, S = qT.shape
    nk = S // KEY_CHUNK
    return pl.pallas_call(
        functools.partial(_dsa_kernel, topk=topk),
        grid=(B, S // Q_TILE),
        in_specs=[
            pl.BlockSpec((1, ATT_Q, Q_TILE), lambda b, i: (b, 0, i)),
            pl.BlockSpec((1, IDX_Q, Q_TILE), lambda b, i: (b, 0, i)),
            pl.BlockSpec((1, SUBLANES, Q_TILE), lambda b, i: (b, 0, i)),
            pl.BlockSpec((1, S, ATT_KV), lambda b, i: (b, 0, 0)),
            pl.BlockSpec((1, S, LANES), lambda b, i: (b, 0, 0)),
            pl.BlockSpec((1, nk, ATT_KV, KEY_CHUNK), lambda b, i: (b, 0, 0, 0)),
        ],
        out_specs=pl.BlockSpec((1, Q_TILE, ATT_Q), lambda b, i: (b, i, 0)),
        out_shape=jax.ShapeDtypeStruct((B, S, ATT_Q), BF16),
        scratch_shapes=[
            pltpu.VMEM((nk, KEY_CHUNK, Q_TILE), F32),
            pltpu.VMEM((nk, KEY_CHUNK, Q_TILE), BF16),
            pltpu.VMEM((ATT_KV, N_HEADS * Q_TILE), BF16),
            pltpu.VMEM((LANES, IDX_HEADS * Q_TILE), BF16),
            pltpu.VMEM((HEAD_DIM, N_HEADS * Q_TILE), F32),
            pltpu.VMEM((1, N_HEADS * Q_TILE), F32),
            pltpu.VMEM((1, N_HEADS * Q_TILE), F32),
        ],
        compiler_params=pltpu.CompilerParams(
            dimension_semantics=("parallel", "arbitrary"), vmem_limit_bytes=VMEM_LIMIT),
        name="dsa",
    )(qT, qiT, wiT, k, ki, vT)


def _post_kernel(attn_ref, sa_ref, pb_ref, x_ref, woa_ref, wout_ref, g1_ref, b1_ref,
                 w1_ref, w2_ref, g2_ref, b2_ref, o_ref, *, alpha):
    ya = _dot(attn_ref[0], woa_ref[...])
    merged = sa_ref[0].astype(F32) * ya + pb_ref[0].astype(F32)
    mix = _dot(merged.astype(BF16), wout_ref[...])
    x1 = _layer_norm(alpha * x_ref[0] + mix, g1_ref[...], b1_ref[...])
    xb = x1.astype(BF16)
    acc = jnp.zeros(x1.shape, F32)
    for cc in range(D_FF // FF_CHUNK):
        h = _dot(xb, w1_ref[:, cc * FF_CHUNK:(cc + 1) * FF_CHUNK])
        h = jnp.square(jnp.maximum(h, 0.0)).astype(BF16)
        acc = acc + _dot(h, w2_ref[cc * FF_CHUNK:(cc + 1) * FF_CHUNK, :])
    o_ref[0] = _layer_norm(alpha * x1 + acc, g2_ref[...], b2_ref[...])


def _post(attn, sa, pb, x, woa, wout, g1, b1, w1, w2, g2, b2, *, alpha):
    B, S, D = x.shape
    tm = POST_TOKENS
    tok = lambda width: pl.BlockSpec((1, tm, width), lambda b, t: (b, t, 0))
    return pl.pallas_call(
        functools.partial(_post_kernel, alpha=alpha),
        grid=(B, S // tm),
        in_specs=[tok(ATT_Q), tok(D), tok(D), tok(D),
                  _resident((ATT_Q, D)), _resident((D, D)), _resident((1, D)), _resident((1, D)),
                  _resident((D, D_FF)), _resident((D_FF, D)), _resident((1, D)), _resident((1, D))],
        out_specs=tok(D),
        out_shape=jax.ShapeDtypeStruct((B, S, D), F32),
        compiler_params=pltpu.CompilerParams(
            dimension_semantics=("parallel", "parallel"), vmem_limit_bytes=VMEM_LIMIT),
        name="post",
    )(attn, sa, pb, x, woa, wout, g1, b1, w1, w2, g2, b2)


def kernel(x, positions, w_in, w_s, b_s, ln_v_g, ln_v_b, w_oa, w_ob, w_out,
           ln1_g, ln1_b, w_ff1, w_ff2, ln2_g, ln2_b):
    B, S, D = x.shape
    depth = w_in.shape[0]
    assert D == D_MODEL and S % max(IN_PROJ_TOKENS, POST_TOKENS, GROUP_KEYS) == 0
    assert w_in.shape[2] == COL_T_END + COLS_S
    alpha = float((2 * depth) ** 0.25)
    topk = min(TOPK_MAX, S // 4)
    cosT, sinT = _rope_tables(positions)
    wt_all = _transposed_in_weights(w_in)
    row2 = lambda v: v.reshape(1, -1)
    for l in range(depth):
        ws = w_in[l, :, COL_T_END:].astype(BF16)
        bs_b = jnp.repeat(b_s[l].T, SGU_WIDTH // SGU_GROUPS, axis=1)
        qT, k, vT, qiT, ki, wiT, sa, pb = _in_proj(
            x, cosT, sinT, wt_all[l], ws, w_ob[l].astype(BF16), w_s[l], bs_b,
            row2(ln_v_g[l]), row2(ln_v_b[l]))
        attn = _dsa(qT, qiT, wiT, k, ki, vT, topk=topk)
        x = _post(attn, sa, pb, x, w_oa[l].astype(BF16), w_out[l].astype(BF16),
                  row2(ln1_g[l]), row2(ln1_b[l]), w_ff1[l].astype(BF16), w_ff2[l].astype(BF16),
                  row2(ln2_g[l]), row2(ln2_b[l]), alpha=alpha)
    return x
```

```python
import functools

import numpy as np
import jax
import jax.numpy as jnp
from jax import lax
from jax.experimental import pallas as pl
from jax.experimental.pallas import tpu as pltpu

F32 = jnp.float32
BF16 = jnp.bfloat16
I32 = jnp.int32

D_MODEL = 1024
N_HEADS = 8
N_KV_HEADS = 2
HEAD_DIM = 64
ROPE_DIM = HEAD_DIM // 4
ROPE_HALF = ROPE_DIM // 2
ROPE_THETA = 500000.0
IDX_HEADS = 4
IDX_DIM = 64
TOPK_MAX = 256
SGU_CHUNK = 128
SGU_GROUPS = 4
SGU_WIDTH = 512
D_FF = 4 * D_MODEL
LN_EPS = 1e-5

ATT_Q = N_HEADS * HEAD_DIM
ATT_KV = N_KV_HEADS * HEAD_DIM
IDX_Q = IDX_HEADS * IDX_DIM
HEADS_PER_KV = N_HEADS // N_KV_HEADS

LANES = 128
SUBLANES = 8
KEY_CHUNK = LANES
Q_TILE = 4 * LANES
GROUP = 4
GROUP_KEYS = GROUP * KEY_CHUNK
IN_PROJ_TOKENS = 512
POST_TOKENS = 512
FF_CHUNK = 512
VMEM_LIMIT = 56 * 1024 * 1024

LOG2E = float(np.log2(np.e))
ATT_SCALE = HEAD_DIM ** -0.5
IDX_SCALE = (IDX_HEADS ** -0.5) * (IDX_DIM ** -0.5)
NEG_BIG = -1e30

KEY_NEG_INF = np.int32(np.array(0xFF800000, np.uint32).view(np.int32) ^ np.int32(0x7FFFFFFF))
KEY_POS_INF = np.int32(0x7F800000)
INT_MIN = np.int32(-2 ** 31)
KEY16_POS_INF = np.int32(0x7F80)
KEY16_NEG_INF = np.int32(-0x7F80 - 1)
BF16_ROWS = 16
FINE_ROUNDS = 17

ROW_Q = 0
ROW_K = ROW_Q + ATT_Q
ROW_V = ROW_K + ATT_KV
ROW_QI = ROW_V + ATT_KV
ROW_KI = ROW_QI + IDX_Q
ROWS_T = ROW_KI + LANES
COL_T_END = ROW_KI + IDX_DIM + IDX_HEADS
COL_U = 0
COL_VG = COL_U + SGU_WIDTH
COL_GA = COL_VG + SGU_WIDTH
COL_GB = COL_GA + D_MODEL
COLS_S = COL_GB + D_MODEL


def _layer_norm(z, g, b):
    mu = jnp.mean(z, axis=-1, keepdims=True)
    zc = z - mu
    var = jnp.mean(zc * zc, axis=-1, keepdims=True)
    return zc * lax.rsqrt(var + LN_EPS) * g + b


def _gelu(z):
    return 0.5 * z * (1.0 + lax.erf(z * np.float32(np.sqrt(0.5))))


def _dot(a, b):
    return jnp.dot(a, b, preferred_element_type=F32)


def _dot_nt(a, b):
    return lax.dot_general(a, b, (((1,), (1,)), ((), ())), preferred_element_type=F32)


def _resident(shape):
    return pl.BlockSpec(shape, lambda *_: (0,) * len(shape), pipeline_mode=pl.Buffered(1))


def _rope_kernel(pos_ref, inv_ref, cos_ref, sin_ref):
    ang = pos_ref[0].astype(F32) * inv_ref[:, 0:1]
    cos_ref[0] = jnp.cos(ang)
    sin_ref[0] = jnp.sin(ang)


def _rope_tables(positions):
    B, S = positions.shape
    inv_freq = ROPE_THETA ** (-jnp.arange(0, ROPE_DIM, 2, dtype=F32) / ROPE_DIM)
    inv_b = jnp.broadcast_to(inv_freq[:, None], (ROPE_HALF, LANES))
    return pl.pallas_call(
        _rope_kernel,
        grid=(B,),
        in_specs=[pl.BlockSpec((1, 1, S), lambda b: (b, 0, 0)),
                  pl.BlockSpec((ROPE_HALF, LANES), lambda b: (0, 0))],
        out_specs=[pl.BlockSpec((1, ROPE_HALF, S), lambda b: (b, 0, 0)),
                   pl.BlockSpec((1, ROPE_HALF, S), lambda b: (b, 0, 0))],
        out_shape=[jax.ShapeDtypeStruct((B, ROPE_HALF, S), F32)] * 2,
        name="rope_tables",
    )(positions.reshape(B, 1, S), inv_b)


def _wt_kernel(w_ref, o_ref):
    o_ref[0] = w_ref[0].T.astype(BF16)


def _transposed_in_weights(w_in):
    depth, d, _ = w_in.shape
    return pl.pallas_call(
        _wt_kernel,
        grid=(depth, ROWS_T // LANES),
        in_specs=[pl.BlockSpec((1, d, LANES), lambda l, c: (l, 0, c))],
        out_specs=pl.BlockSpec((1, LANES, d), lambda l, c: (l, c, 0)),
        out_shape=jax.ShapeDtypeStruct((depth, ROWS_T, d), BF16),
        name="w_in_transpose",
    )(w_in)


def _rope_rows(t, n_heads, c, s):
    pieces = []
    for h in range(n_heads):
        base = h * HEAD_DIM
        x1 = t[base:base + ROPE_HALF]
        x2 = t[base + ROPE_HALF:base + ROPE_DIM]
        pieces += [x1 * c - x2 * s, x2 * c + x1 * s, t[base + ROPE_DIM:base + HEAD_DIM]]
    return jnp.concatenate(pieces, axis=0)


def _in_proj_kernel(x_ref, cos_ref, sin_ref, wt_ref, ws_ref, wob_ref, wsgu_ref, bs_ref, lng_ref, lnb_ref,
                    qT_ref, k_ref, vT_ref, qiT_ref, ki_ref, wiT_ref, sa_ref, pb_ref, *, tm):
    xb = x_ref[0].astype(BF16)
    c = cos_ref[0]
    s = sin_ref[0]

    def proj_t(lo, hi):
        return _dot_nt(wt_ref[lo:hi, :], xb)

    qT_ref[0] = (_rope_rows(proj_t(ROW_Q, ROW_K), N_HEADS, c, s) * (ATT_SCALE * LOG2E)).astype(BF16)
    k_ref[0] = _rope_rows(proj_t(ROW_K, ROW_V), N_KV_HEADS, c, s).T.astype(BF16)
    vT = proj_t(ROW_V, ROW_QI).astype(BF16)
    for cc in range(tm // KEY_CHUNK):
        vT_ref[0, cc] = vT[:, cc * KEY_CHUNK:(cc + 1) * KEY_CHUNK]
    qiT_ref[0] = _rope_rows(proj_t(ROW_QI, ROW_KI), IDX_HEADS, c, s).astype(BF16)
    kiT = proj_t(ROW_KI, ROWS_T)
    wiT_ref[0] = kiT[IDX_DIM:IDX_DIM + SUBLANES] * IDX_SCALE
    kiT = jnp.concatenate([_rope_rows(kiT[0:IDX_DIM], 1, c, s), kiT[IDX_DIM:]], axis=0)
    ki_ref[0] = kiT.T.astype(BF16)

    u = _gelu(_dot(xb, ws_ref[:, COL_U:COL_VG]))
    vg = _gelu(_dot(xb, ws_ref[:, COL_VG:COL_GA]))
    vn = _layer_norm(vg, lng_ref[...], lnb_ref[...]).astype(BF16)
    row = lax.broadcasted_iota(I32, (SGU_CHUNK, SGU_CHUNK), 0)
    col = lax.broadcasted_iota(I32, (SGU_CHUNK, SGU_CHUNK), 1)
    w_tril = [jnp.where(row >= col, wsgu_ref[g], 0.0).astype(BF16) for g in range(SGU_GROUPS)]
    gd = SGU_WIDTH // SGU_GROUPS
    chunks = []
    for cc in range(tm // SGU_CHUNK):
        r0 = cc * SGU_CHUNK
        cols = [_dot(w_tril[g], vn[r0:r0 + SGU_CHUNK, g * gd:(g + 1) * gd]) for g in range(SGU_GROUPS)]
        chunks.append(jnp.concatenate(cols, axis=1) + bs_ref[...])
    sgu = u * jnp.concatenate(chunks, axis=0)
    yb = _dot(sgu.astype(BF16), wob_ref[...])

    sa_ref[0] = jax.nn.sigmoid(_dot(xb, ws_ref[:, COL_GA:COL_GB])).astype(BF16)
    pb_ref[0] = (jax.nn.sigmoid(_dot(xb, ws_ref[:, COL_GB:COLS_S])) * yb).astype(BF16)


def _in_proj(x, cosT, sinT, wt, ws, wob, wsgu, bs_b, lng, lnb):
    B, S, D = x.shape
    tm = IN_PROJ_TOKENS
    out_shape = [
        jax.ShapeDtypeStruct((B, ATT_Q, S), BF16),
        jax.ShapeDtypeStruct((B, S, ATT_KV), BF16),
        jax.ShapeDtypeStruct((B, S // KEY_CHUNK, ATT_KV, KEY_CHUNK), BF16),
        jax.ShapeDtypeStruct((B, IDX_Q, S), BF16),
        jax.ShapeDtypeStruct((B, S, LANES), BF16),
        jax.ShapeDtypeStruct((B, SUBLANES, S), F32),
        jax.ShapeDtypeStruct((B, S, D), BF16),
        jax.ShapeDtypeStruct((B, S, D), BF16),
    ]
    out_specs = [
        pl.BlockSpec((1, ATT_Q, tm), lambda b, t: (b, 0, t)),
        pl.BlockSpec((1, tm, ATT_KV), lambda b, t: (b, t, 0)),
        pl.BlockSpec((1, tm // KEY_CHUNK, ATT_KV, KEY_CHUNK), lambda b, t: (b, t, 0, 0)),
        pl.BlockSpec((1, IDX_Q, tm), lambda b, t: (b, 0, t)),
        pl.BlockSpec((1, tm, LANES), lambda b, t: (b, t, 0)),
        pl.BlockSpec((1, SUBLANES, tm), lambda b, t: (b, 0, t)),
        pl.BlockSpec((1, tm, D), lambda b, t: (b, t, 0)),
        pl.BlockSpec((1, tm, D), lambda b, t: (b, t, 0)),
    ]
    in_specs = [
        pl.BlockSpec((1, tm, D), lambda b, t: (b, t, 0)),
        pl.BlockSpec((1, ROPE_HALF, tm), lambda b, t: (b, 0, t)),
        pl.BlockSpec((1, ROPE_HALF, tm), lambda b, t: (b, 0, t)),
        _resident((ROWS_T, D)),
        _resident((D, COLS_S)),
        _resident((SGU_WIDTH, D)),
        _resident((SGU_GROUPS, SGU_CHUNK, SGU_CHUNK)),
        _resident((SGU_CHUNK, SGU_WIDTH)),
        _resident((1, SGU_WIDTH)),
        _resident((1, SGU_WIDTH)),
    ]
    return pl.pallas_call(
        functools.partial(_in_proj_kernel, tm=tm),
        grid=(B, S // tm),
        in_specs=in_specs,
        out_specs=out_specs,
        out_shape=out_shape,
        compiler_params=pltpu.CompilerParams(
            dimension_semantics=("parallel", "parallel"), vmem_limit_bytes=VMEM_LIMIT),
        name="in_proj",
    )(x, cosT, sinT, wt, ws, wob, wsgu, bs_b, lng, lnb)


def _sum_rows_to_vreg(v):
    parts = [v[r:r + SUBLANES] for r in range(0, v.shape[0], SUBLANES)]
    while len(parts) > 1:
        nxt = [parts[a] + parts[a + 1] for a in range(0, len(parts) - 1, 2)]
        if len(parts) % 2:
            nxt.append(parts[-1])
        parts = nxt
    return parts[0]


def _key_to_float(key):
    key = jnp.clip(key, KEY_NEG_INF, KEY_POS_INF)
    return pltpu.bitcast(key ^ ((key >> 31) & np.int32(0x7FFFFFFF)), F32)


def _float_to_key(x):
    bits = pltpu.bitcast(x, I32)
    return bits ^ ((bits >> 31) & np.int32(0x7FFFFFFF))


def _key16_to_float(key16):
    key16 = jnp.clip(key16, KEY16_NEG_INF, KEY16_POS_INF)
    bits16 = key16 ^ ((key16 >> 15) & np.int32(0x7FFF))
    return pltpu.bitcast(lax.shift_left(bits16, np.int32(16)), F32)


def _dsa_kernel(qT_ref, qiT_ref, wiT_ref, k_ref, ki_ref, vT_ref, o_ref,
                sc_sc, sc16_sc, rq_sc, ri_sc, acc_sc, m_sc, l_sc, *, topk):
    i = pl.program_id(1)
    n_groups = ((i + 1) * Q_TILE + GROUP_KEYS - 1) // GROUP_KEYS

    rq_sc[...] = jnp.zeros(rq_sc.shape, BF16)
    for h in range(N_HEADS):
        g = h // HEADS_PER_KV
        rq_sc[g * HEAD_DIM:(g + 1) * HEAD_DIM, h * Q_TILE:(h + 1) * Q_TILE] = \
            qT_ref[0, h * HEAD_DIM:(h + 1) * HEAD_DIM, :]
    ri_sc[...] = jnp.zeros(ri_sc.shape, BF16)
    for h in range(IDX_HEADS):
        ri_sc[0:IDX_DIM, h * Q_TILE:(h + 1) * Q_TILE] = qiT_ref[0, h * IDX_DIM:(h + 1) * IDX_DIM, :]

    w = wiT_ref[0]
    row_c = lax.broadcasted_iota(I32, (KEY_CHUNK, Q_TILE), 0)
    col_c = lax.broadcasted_iota(I32, (KEY_CHUNK, Q_TILE), 1)

    def group_rows(jj):
        return pl.ds(pl.multiple_of(jj * GROUP_KEYS, GROUP_KEYS), GROUP_KEYS)

    def load_group(jj):
        return jnp.concatenate([sc_sc[jj * GROUP + cc] for cc in range(GROUP)], axis=0)

    def score_body(jj, carry):
        for cc in range(GROUP):
            ch = jj * GROUP + cc
            kc = ki_ref[0, pl.ds(pl.multiple_of(ch * KEY_CHUNK, KEY_CHUNK), KEY_CHUNK), :]
            lg = _dot(kc, ri_sc[...])
            sc = w[0:1, :] * jnp.maximum(lg[:, 0:Q_TILE], 0.0)
            for h in range(1, IDX_HEADS):
                sc = sc + w[h:h + 1, :] * jnp.maximum(lg[:, h * Q_TILE:(h + 1) * Q_TILE], 0.0)
            sc = jnp.where(ch * KEY_CHUNK + row_c <= i * Q_TILE + col_c, sc, -jnp.inf)
            sc_sc[ch] = sc
            sc16_sc[ch] = sc.astype(BF16)
        return carry

    lax.fori_loop(0, n_groups, score_body, 0)

    def count_pass(pred):
        def body(jj, acc):
            for cc in range(GROUP):
                blk = sc_sc[jj * GROUP + cc]
                acc = acc + _sum_rows_to_vreg(jnp.where(pred(blk), 1, 0).astype(I32))
            return acc
        acc = lax.fori_loop(0, n_groups, body, jnp.zeros((SUBLANES, Q_TILE), I32))
        return jnp.sum(acc, axis=0, keepdims=True)

    def count16_pass(cand_f):
        cand = jnp.broadcast_to(cand_f, (BF16_ROWS, Q_TILE)).astype(BF16)
        one = jnp.ones((BF16_ROWS, Q_TILE), BF16)
        zero = jnp.zeros((BF16_ROWS, Q_TILE), BF16)

        def body(jj, acc):
            for cc in range(GROUP):
                blk = sc16_sc[jj * GROUP + cc]
                parts = [jnp.where(blk[r * BF16_ROWS:(r + 1) * BF16_ROWS] >= cand, one, zero)
                         for r in range(KEY_CHUNK // BF16_ROWS)]
                while len(parts) > 1:
                    parts = [parts[a] + parts[a + 1] for a in range(0, len(parts), 2)]
                acc = acc + parts[0]
            return acc
        acc = lax.fori_loop(0, n_groups, body, zero)
        return jnp.sum(acc.astype(F32), axis=0, keepdims=True)

    def coarse_body(r, t16):
        cand = t16 + lax.shift_left(np.int32(1), 15 - r)
        total = count16_pass(_key16_to_float(cand))
        return jnp.where(total >= topk, cand, t16)

    t16 = lax.fori_loop(0, 16, coarse_body, jnp.full((1, Q_TILE), -2 ** 15, I32))
    k32 = _float_to_key(_key16_to_float(t16))
    lo0 = jnp.maximum(k32, KEY_NEG_INF + 2 ** 15 + 1) - (2 ** 15 + 1)
    hi0 = jnp.minimum(k32, KEY_POS_INF - 2 ** 16 - 1) + (2 ** 16 + 1)

    def fine_body(r, lohi):
        lo, hi = lohi
        mid = lo + lax.shift_right_arithmetic(hi - lo, np.int32(1))
        mid_f = _key_to_float(mid)
        ge = count_pass(lambda blk: blk >= mid_f) >= topk
        return jnp.where(ge, mid, lo), jnp.where(ge, hi, mid)

    t_key, _ = lax.fori_loop(0, FINE_ROUNDS, fine_body, (lo0, hi0))
    t = _key_to_float(t_key)
    n_gt = count_pass(lambda blk: blk > t)
    n_ties = (topk - n_gt).astype(F32)

    tri = jnp.where(lax.broadcasted_iota(I32, (KEY_CHUNK, KEY_CHUNK), 0)
                    >= lax.broadcasted_iota(I32, (KEY_CHUNK, KEY_CHUNK), 1), 1.0, 0.0).astype(BF16)

    def mask_body(jj, tie_carry):
        for cc in range(GROUP):
            ch = jj * GROUP + cc
            sc = sc_sc[ch]
            eq = sc == t
            pref = _dot(tri, jnp.where(eq, 1.0, 0.0).astype(BF16)) + tie_carry
            tie_carry = pref[KEY_CHUNK - 1:KEY_CHUNK, :]
            sel = ((sc > t) | (eq & (pref <= n_ties))) & (ch * KEY_CHUNK + row_c <= i * Q_TILE + col_c)
            sc_sc[ch] = jnp.where(sel, 0.0, NEG_BIG)
        return tie_carry

    lax.fori_loop(0, n_groups, mask_body, jnp.zeros((1, Q_TILE), F32))

    m_sc[...] = jnp.full(m_sc.shape, NEG_BIG, F32)
    l_sc[...] = jnp.zeros(l_sc.shape, F32)
    acc_sc[...] = jnp.zeros(acc_sc.shape, F32)
    kv_cols = HEADS_PER_KV * Q_TILE

    def attn_body(jj, carry):
        bias = load_group(jj)
        s = _dot(k_ref[0, group_rows(jj), :], rq_sc[...])
        for g in range(N_KV_HEADS):
            ps, alphas = [], []
            for h in range(g * HEADS_PER_KV, (g + 1) * HEADS_PER_KV):
                cs = slice(h * Q_TILE, (h + 1) * Q_TILE)
                sm = s[:, cs] + bias
                m_old = m_sc[:, cs]
                m_new = jnp.maximum(m_old, jnp.max(sm, axis=0, keepdims=True))
                alpha = jnp.exp2(m_old - m_new)
                p = jnp.exp2(sm - m_new)
                l_sc[:, cs] = alpha * l_sc[:, cs] + jnp.sum(p, axis=0, keepdims=True)
                m_sc[:, cs] = m_new
                ps.append(p.astype(BF16))
                alphas.append(alpha)
            gs = slice(g * kv_cols, (g + 1) * kv_cols)
            vt = jnp.concatenate([vT_ref[0, jj * GROUP + cc, g * HEAD_DIM:(g + 1) * HEAD_DIM, :]
                                  for cc in range(GROUP)], axis=1)
            pv = _dot(vt, jnp.concatenate(ps, axis=1))
            acc_sc[:, gs] = jnp.concatenate(alphas, axis=1) * acc_sc[:, gs] + pv
        return carry

    lax.fori_loop(0, n_groups, attn_body, 0)

    out_t = acc_sc[...] / l_sc[...]
    for pr in range(N_HEADS // 2):
        pair = jnp.concatenate([out_t[:, (2 * pr) * Q_TILE:(2 * pr + 1) * Q_TILE],
                                out_t[:, (2 * pr + 1) * Q_TILE:(2 * pr + 2) * Q_TILE]], axis=0)
        o_ref[0, :, pr * LANES:(pr + 1) * LANES] = pair.T.astype(BF16)


def _dsa(qT, qiT, wiT, k, ki, vT, *, topk):
    B, _, S = qT.shape
    nk = S // KEY_CHUNK
    return pl.pallas_call(
        functools.partial(_dsa_kernel, topk=topk),
        grid=(B, S // Q_TILE),
        in_specs=[
            pl.BlockSpec((1, ATT_Q, Q_TILE), lambda b, i: (b, 0, i)),
            pl.BlockSpec((1, IDX_Q, Q_TILE), lambda b, i: (b, 0, i)),
            pl.BlockSpec((1, SUBLANES, Q_TILE), lambda b, i: (b, 0, i)),
            pl.BlockSpec((1, S, ATT_KV), lambda b, i: (b, 0, 0)),
            pl.BlockSpec((1, S, LANES), lambda b, i: (b, 0, 0)),
            pl.BlockSpec((1, nk, ATT_KV, KEY_CHUNK), lambda b, i: (b, 0, 0, 0)),
        ],
        out_specs=pl.BlockSpec((1, Q_TILE, ATT_Q), lambda b, i: (b, i, 0)),
        out_shape=jax.ShapeDtypeStruct((B, S, ATT_Q), BF16),
        scratch_shapes=[
            pltpu.VMEM((nk, KEY_CHUNK, Q_TILE), F32),
            pltpu.VMEM((nk, KEY_CHUNK, Q_TILE), BF16),
            pltpu.VMEM((ATT_KV, N_HEADS * Q_TILE), BF16),
            pltpu.VMEM((LANES, IDX_HEADS * Q_TILE), BF16),
            pltpu.VMEM((HEAD_DIM, N_HEADS * Q_TILE), F32),
            pltpu.VMEM((1, N_HEADS * Q_TILE), F32),
            pltpu.VMEM((1, N_HEADS * Q_TILE), F32),
        ],
        compiler_params=pltpu.CompilerParams(
            dimension_semantics=("parallel", "arbitrary"), vmem_limit_bytes=VMEM_LIMIT),
        name="dsa",
    )(qT, qiT, wiT, k, ki, vT)


def _post_kernel(attn_ref, sa_ref, pb_ref, x_ref, woa_ref, wout_ref, g1_ref, b1_ref,
                 w1_ref, w2_ref, g2_ref, b2_ref, o_ref, *, alpha):
    ya = _dot(attn_ref[0], woa_ref[...])
    merged = sa_ref[0].astype(F32) * ya + pb_ref[0].astype(F32)
    mix = _dot(merged.astype(BF16), wout_ref[...])
    x1 = _layer_norm(alpha * x_ref[0] + mix, g1_ref[...], b1_ref[...])
    xb = x1.astype(BF16)
    acc = jnp.zeros(x1.shape, F32)
    for cc in range(D_FF // FF_CHUNK):
        h = _dot(xb, w1_ref[:, cc * FF_CHUNK:(cc + 1) * FF_CHUNK])
        h = jnp.square(jnp.maximum(h, 0.0)).astype(BF16)
        acc = acc + _dot(h, w2_ref[cc * FF_CHUNK:(cc + 1) * FF_CHUNK, :])
    o_ref[0] = _layer_norm(alpha * x1 + acc, g2_ref[...], b2_ref[...])


def _post(attn, sa, pb, x, woa, wout, g1, b1, w1, w2, g2, b2, *, alpha):
    B, S, D = x.shape
    tm = POST_TOKENS
    tok = lambda width: pl.BlockSpec((1, tm, width), lambda b, t: (b, t, 0))
    return pl.pallas_call(
        functools.partial(_post_kernel, alpha=alpha),
        grid=(B, S // tm),
        in_specs=[tok(ATT_Q), tok(D), tok(D), tok(D),
                  _resident((ATT_Q, D)), _resident((D, D)), _resident((1, D)), _resident((1, D)),
                  _resident((D, D_FF)), _resident((D_FF, D)), _resident((1, D)), _resident((1, D))],
        out_specs=tok(D),
        out_shape=jax.ShapeDtypeStruct((B, S, D), F32),
        compiler_params=pltpu.CompilerParams(
            dimension_semantics=("parallel", "parallel"), vmem_limit_bytes=VMEM_LIMIT),
        name="post",
    )(attn, sa, pb, x, woa, wout, g1, b1, w1, w2, g2, b2)


def kernel(x, positions, w_in, w_s, b_s, ln_v_g, ln_v_b, w_oa, w_ob, w_out,
           ln1_g, ln1_b, w_ff1, w_ff2, ln2_g, ln2_b):
    B, S, D = x.shape
    depth = w_in.shape[0]
    assert D == D_MODEL and S % max(IN_PROJ_TOKENS, POST_TOKENS, GROUP_KEYS) == 0
    assert w_in.shape[2] == COL_T_END + COLS_S
    alpha = float((2 * depth) ** 0.25)
    topk = min(TOPK_MAX, S // 4)
    cosT, sinT = _rope_tables(positions)
    wt_all = _transposed_in_weights(w_in)
    row2 = lambda v: v.reshape(1, -1)
    for l in range(depth):
        ws = w_in[l, :, COL_T_END:].astype(BF16)
        bs_b = jnp.repeat(b_s[l].T, SGU_WIDTH // SGU_GROUPS, axis=1)
        qT, k, vT, qiT, ki, wiT, sa, pb = _in_proj(
            x, cosT, sinT, wt_all[l], ws, w_ob[l].astype(BF16), w_s[l], bs_b,
            row2(ln_v_g[l]), row2(ln_v_b[l]))
        attn = _dsa(qT, qiT, wiT, k, ki, vT, topk=topk)
        x = _post(attn, sa, pb, x, w_oa[l].astype(BF16), w_out[l].astype(BF16),
                  row2(ln1_g[l]), row2(ln1_b[l]), w_ff1[l].astype(BF16), w_ff2[l].astype(BF16),
                  row2(ln2_g[l]), row2(ln2_b[l]), alpha=alpha)
    return x
```
